```python
import jax
import jax.numpy as jnp
from jax import lax
import numpy as np

D_MODEL = 2048
BATCH = 2
SEQ = 8192
DEPTH = 4

N_MIXERS = 2
N_CONV_LAYERS = (DEPTH + 1) // 2
N_GLA_LAYERS = DEPTH // 2
CONV_WIDTH = 31
GLA_HEADS = 4
GLA_DK = D_MODEL // 2
GLA_DV = D_MODEL
GLA_HEAD_K = GLA_DK // GLA_HEADS
GLA_HEAD_V = GLA_DV // GLA_HEADS
GLA_GATE_RANK = 16
GLA_TAU = 16.0
GLA_CHUNK = 64
GLA_IN_WIDTH = 2 * GLA_DK + 2 * GLA_DV + GLA_GATE_RANK
N_GROUPS = 4
EXPERTS_PER_GROUP = 8
N_EXPERTS = N_GROUPS * EXPERTS_PER_GROUP
TOP_K_IN_GROUP = 2
D_EXPERT = D_MODEL // 4
MOE_BLOCK = 128
N_ADA = 6
EPS = 1e-6

kernel_name = "hybrid_conv_gla_hmoe_adaln"


def rms_norm(x, gain):
    xf = x.astype(jnp.float32)
    y = xf * lax.rsqrt(jnp.mean(xf * xf, axis=-1, keepdims=True) + EPS)
    return (y * gain.astype(jnp.float32)).astype(x.dtype)


def conv_module(h, w_pw1, b_pw1, w_dw, b_dw, norm_g, w_pw2, b_pw2):
    u = h @ w_pw1 + b_pw1
    a, b = jnp.split(u, 2, axis=-1)
    u = a * jax.nn.sigmoid(b)
    u = lax.conv_general_dilated(
        u, w_dw[:, None, :].astype(u.dtype), window_strides=(1,),
        padding=[(CONV_WIDTH - 1, 0)], dimension_numbers=("NWC", "WIO", "NWC"),
        feature_group_count=D_MODEL) + b_dw
    u = jax.nn.silu(rms_norm(u, norm_g))
    return u @ w_pw2 + b_pw2


def gla_mixer(h, w_in, w_a2, b_a, head_g, w_o):
    B, S, _ = h.shape
    nc = S // GLA_CHUNK
    proj = h @ w_in
    q, k, v, r, a_lr = jnp.split(
        proj, [GLA_DK, 2 * GLA_DK, 2 * GLA_DK + GLA_DV, 2 * GLA_DK + 2 * GLA_DV], axis=-1)
    log_a = jax.nn.log_sigmoid((a_lr @ w_a2 + b_a).astype(jnp.float32)) / GLA_TAU

    def to_chunks(t, dh):
        return t.reshape(B, nc, GLA_CHUNK, GLA_HEADS, dh).transpose(1, 0, 3, 2, 4).astype(jnp.float32)

    qc = to_chunks(q, GLA_HEAD_K) * (GLA_HEAD_K ** -0.5)
    kc = to_chunks(k, GLA_HEAD_K)
    vc = to_chunks(v, GLA_HEAD_V)
    lac = to_chunks(log_a, GLA_HEAD_K)
    causal = jnp.tril(jnp.ones((GLA_CHUNK, GLA_CHUNK), dtype=bool))

    def step(state, inp):
        q_, k_, v_, la = inp
        cum = jnp.cumsum(la, axis=-2)
        cum_last = cum[..., -1:, :]
        q_dec = q_ * jnp.exp(cum)
        k_inv = k_ * jnp.exp(-cum)
        att = jnp.where(causal, jnp.einsum("bhik,bhjk->bhij", q_dec, k_inv), 0.0)
        o = (jnp.einsum("bhij,bhjv->bhiv", att, v_)
             + jnp.einsum("bhik,bhkv->bhiv", q_dec, state))
        k_end = k_ * jnp.exp(cum_last - cum)
        state = (jnp.exp(cum_last)[..., 0, :, None] * state
                 + jnp.einsum("bhjk,bhjv->bhkv", k_end, v_))
        return state, o

    state0 = jnp.zeros((B, GLA_HEADS, GLA_HEAD_K, GLA_HEAD_V), jnp.float32)
    _, o = lax.scan(step, state0, (qc, kc, vc, lac))
    o = rms_norm(o, head_g[:, None, :])
    o = o.transpose(1, 0, 3, 2, 4).reshape(B, S, GLA_DV).astype(h.dtype)
    return (jax.nn.silu(r) * o) @ w_o


def hier_moe(h, w_group, b_group, w_expert, b_expert, w13, w2):
    B, S, D = h.shape
    T = B * S
    A = T * TOP_K_IN_GROUP
    xt = h.reshape(T, D)
    g_prob = jax.nn.softmax((xt @ w_group).astype(jnp.float32) + b_group, axis=-1)
    g_val, g_idx = lax.top_k(g_prob, 1)
    e_logits = ((xt @ w_expert).astype(jnp.float32) + b_expert).reshape(T, N_GROUPS, EXPERTS_PER_GROUP)
    e_logits = jnp.take_along_axis(e_logits, g_idx[:, :, None], axis=1)[:, 0]
    e_val, e_idx = lax.top_k(jax.nn.softmax(e_logits, axis=-1), TOP_K_IN_GROUP)
    weight = g_val * (e_val / jnp.sum(e_val, axis=-1, keepdims=True))
    expert = g_idx * EXPERTS_PER_GROUP + e_idx

    flat_e = expert.reshape(A)
    flat_w = weight.reshape(A)
    flat_tok = jnp.repeat(jnp.arange(T, dtype=jnp.int32), TOP_K_IN_GROUP)
    order = jnp.argsort(flat_e)
    se = flat_e[order]
    tok_sorted = flat_tok[order]
    counts = jnp.bincount(flat_e, length=N_EXPERTS)
    starts = jnp.cumsum(counts) - counts
    padded = (counts + MOE_BLOCK - 1) // MOE_BLOCK * MOE_BLOCK
    pad_ends = jnp.cumsum(padded)
    pad_starts = pad_ends - padded
    dest = pad_starts[se] + jnp.arange(A, dtype=jnp.int32) - starts[se]
    n_blocks = A // MOE_BLOCK + N_EXPERTS
    xs = jnp.zeros((n_blocks * MOE_BLOCK, D), h.dtype).at[dest].set(xt[tok_sorted])
    block_expert = jnp.minimum(
        jnp.searchsorted(pad_ends, jnp.arange(n_blocks, dtype=jnp.int32) * MOE_BLOCK, side="right"),
        N_EXPERTS - 1)

    def expert_block(args):
        xb, e = args
        a, g = jnp.split(xb @ w13[e], 2, axis=-1)
        return (jax.nn.silu(a) * g) @ w2[e]

    ys = lax.map(expert_block, (xs.reshape(n_blocks, MOE_BLOCK, D), block_expert))
    ys = ys.reshape(n_blocks * MOE_BLOCK, D)
    contrib = ys[dest] * flat_w[order][:, None].astype(ys.dtype)
    out = jnp.zeros((T, D), ys.dtype).at[tok_sorted].add(contrib)
    return out.reshape(B, S, D)


def setup_inputs(seed: int = 0) -> dict:
    key = jax.random.key(seed)
    ks = jax.random.split(key, 25)
    L, Lc, Lg, D = DEPTH, N_CONV_LAYERS, N_GLA_LAYERS, D_MODEL

    def nrm(k, shape, s):
        return jax.random.normal(k, shape, jnp.float32) * s

    def gain(k, shape):
        return 1.0 + 0.01 * jax.random.normal(k, shape, jnp.float32)

    return {
        "x": nrm(ks[0], (BATCH, SEQ, D), 1.0),
        "c": nrm(ks[1], (BATCH, D), 1.0),
        "ada_w": nrm(ks[2], (L, D, N_ADA * D), 0.5 * D ** -0.5),
        "ada_b": nrm(ks[3], (L, N_ADA * D), 0.01),
        "mix_norm_g": gain(ks[4], (L, D)),
        "ffn_norm_g": gain(ks[5], (L, D)),
        "conv_w_pw1": nrm(ks[6], (Lc, D, 2 * D), D ** -0.5),
        "conv_b_pw1": nrm(ks[7], (Lc, 2 * D), 0.01),
        "conv_w_dw": nrm(ks[8], (Lc, CONV_WIDTH, D), CONV_WIDTH ** -0.5),
        "conv_b_dw": nrm(ks[9], (Lc, D), 0.01),
        "conv_norm_g": gain(ks[10], (Lc, D)),
        "conv_w_pw2": nrm(ks[11], (Lc, D, D), D ** -0.5),
        "conv_b_pw2": nrm(ks[12], (Lc, D), 0.01),
        "gla_w_in": nrm(ks[13], (Lg, D, GLA_IN_WIDTH), D ** -0.5),
        "gla_w_a2": nrm(ks[14], (Lg, GLA_GATE_RANK, GLA_DK), GLA_GATE_RANK ** -0.5),
        "gla_b_a": nrm(ks[15], (Lg, GLA_DK), 0.01),
        "gla_head_g": gain(ks[16], (Lg, GLA_HEADS, GLA_HEAD_V)),
        "gla_w_o": nrm(ks[17], (Lg, GLA_DV, D), GLA_DV ** -0.5),
        "moe_w_group": nrm(ks[18], (L, D, N_GROUPS), D ** -0.5),
        "moe_b_group": nrm(ks[19], (L, N_GROUPS), 0.01),
        "moe_w_expert": nrm(ks[20], (L, D, N_EXPERTS), D ** -0.5),
        "moe_b_expert": nrm(ks[21], (L, N_EXPERTS), 0.01),
        "moe_w13": nrm(ks[22], (L, N_EXPERTS, D, 2 * D_EXPERT), D ** -0.5),
        "moe_w2": nrm(ks[23], (L, N_EXPERTS, D_EXPERT, D), D_EXPERT ** -0.5),
        "final_norm_g": gain(ks[24], (D,)),
    }


def reference(x, c, ada_w, ada_b, mix_norm_g, ffn_norm_g,
              conv_w_pw1, conv_b_pw1, conv_w_dw, conv_b_dw, conv_norm_g, conv_w_pw2, conv_b_pw2,
              gla_w_in, gla_w_a2, gla_b_a, gla_head_g, gla_w_o,
              moe_w_group, moe_b_group, moe_w_expert, moe_b_expert, moe_w13, moe_w2,
              final_norm_g):
    cond = jax.nn.silu(c)
    for i in range(DEPTH):
        mod = cond @ ada_w[i] + ada_b[i]
        sh1, sc1, g1, sh2, sc2, g2 = [m[:, None, :] for m in jnp.split(mod, N_ADA, axis=-1)]
        h = rms_norm(x, mix_norm_g[i]) * (1.0 + sc1) + sh1
        j = i // N_MIXERS
        if i % N_MIXERS == 0:
            y = conv_module(h, conv_w_pw1[j], conv_b_pw1[j], conv_w_dw[j], conv_b_dw[j],
                            conv_norm_g[j], conv_w_pw2[j], conv_b_pw2[j])
        else:
            y = gla_mixer(h, gla_w_in[j], gla_w_a2[j], gla_b_a[j], gla_head_g[j], gla_w_o[j])
        x = x + g1 * y
        h = rms_norm(x, ffn_norm_g[i]) * (1.0 + sc2) + sh2
        x = x + g2 * hier_moe(h, moe_w_group[i], moe_b_group[i], moe_w_expert[i],
                              moe_b_expert[i], moe_w13[i], moe_w2[i])
    return rms_norm(x, final_norm_g)
```

```python
import functools

import jax
import jax.numpy as jnp
from jax import lax
from jax.experimental import pallas as pl
from jax.experimental.pallas import tpu as pltpu

EPS = 1e-6
CONV_WIDTH = 31
CONV_HALO = 32
GLA_HEADS = 4
GLA_GATE_RANK = 16
GLA_TAU = 16.0
GLA_CHUNK = 64
N_GROUPS = 4
EXPERTS_PER_GROUP = 8
N_EXPERTS = N_GROUPS * EXPERTS_PER_GROUP
MOE_BLOCK = 128
N_ADA = 6
LANES = 128
VMEM_LIMIT = 56 * 1024 * 1024

F32 = jnp.float32
BF16 = jnp.bfloat16


def _params(sem):
    return pltpu.CompilerParams(dimension_semantics=sem, vmem_limit_bytes=VMEM_LIMIT)


def _norm_mod(x, gain, scale, shift):
    ms = jnp.mean(x * x, axis=-1, keepdims=True)
    return x * lax.rsqrt(ms + EPS) * gain * (1.0 + scale) + shift


def _silu(x):
    return x * jax.nn.sigmoid(x)


def _ada_kernel(c_ref, w_ref, b_ref, o_ref):
    cond = _silu(c_ref[...])
    o_ref[...] = jnp.dot(cond.astype(BF16), w_ref[...].astype(BF16),
                         preferred_element_type=F32) + b_ref[...]


def _ada_mod(c, ada_w, ada_b):
    L, D, N = ada_w.shape
    B = c.shape[0]
    rows = 8
    c_pad = jnp.zeros((rows, D), F32).at[:B].set(c)
    tn = 1024
    out = pl.pallas_call(
        _ada_kernel,
        grid=(L, N // tn),
        in_specs=[pl.BlockSpec((rows, D), lambda l, n: (0, 0)),
                  pl.BlockSpec((None, D, tn), lambda l, n: (l, 0, n)),
                  pl.BlockSpec((None, 1, tn), lambda l, n: (l, 0, n))],
        out_specs=pl.BlockSpec((None, rows, tn), lambda l, n: (l, 0, n)),
        out_shape=jax.ShapeDtypeStruct((L, rows, N), F32),
        compiler_params=_params(("arbitrary", "arbitrary")),
        name="ada_mod",
    )(c_pad, ada_w, ada_b.reshape(L, 1, N))
    return out[:, :B]


def _proj_kernel(x_ref, g_ref, sc_ref, sh_ref, w_ref, o_ref, h_scr):
    @pl.when(pl.program_id(2) == 0)
    def _():
        h_scr[...] = _norm_mod(x_ref[...], g_ref[...], sc_ref[...], sh_ref[...]).astype(BF16)
    o_ref[...] = jnp.dot(h_scr[...], w_ref[...], preferred_element_type=F32).astype(o_ref.dtype)


def _norm_proj(x, gain, scale, shift, w, tm, tn):
    B, S, D = x.shape
    N = w.shape[1]
    return pl.pallas_call(
        _proj_kernel,
        grid=(B, S // tm, N // tn),
        in_specs=[pl.BlockSpec((None, tm, D), lambda b, s, n: (b, s, 0)),
                  pl.BlockSpec((1, D), lambda b, s, n: (0, 0)),
                  pl.BlockSpec((None, 1, D), lambda b, s, n: (b, 0, 0)),
                  pl.BlockSpec((None, 1, D), lambda b, s, n: (b, 0, 0)),
                  pl.BlockSpec((D, tn), lambda b, s, n: (0, n))],
        out_specs=pl.BlockSpec((None, tm, tn), lambda b, s, n: (b, s, n)),
        out_shape=jax.ShapeDtypeStruct((B, S, N), BF16),
        scratch_shapes=[pltpu.VMEM((tm, D), BF16)],
        compiler_params=_params(("arbitrary", "arbitrary", "arbitrary")),
        name="norm_proj",
    )(x, gain, scale, shift, w)


def _glu_proj_kernel(x_ref, g_ref, sc_ref, sh_ref, wa_ref, wb_ref, ba_ref, bb_ref, o_ref, h_scr):
    @pl.when(pl.program_id(2) == 0)
    def _():
        h_scr[...] = _norm_mod(x_ref[...], g_ref[...], sc_ref[...], sh_ref[...]).astype(BF16)
    h = h_scr[...]
    a = jnp.dot(h, wa_ref[...], preferred_element_type=F32) + ba_ref[...]
    b = jnp.dot(h, wb_ref[...], preferred_element_type=F32) + bb_ref[...]
    o_ref[...] = (a * jax.nn.sigmoid(b)).astype(o_ref.dtype)


def _norm_glu_proj(x, gain, scale, shift, wa, wb, ba, bb, tm, tn):
    B, S, D = x.shape
    N = wa.shape[1]
    vec = pl.BlockSpec((None, 1, D), lambda b, s, n: (b, 0, 0))
    return pl.pallas_call(
        _glu_proj_kernel,
        grid=(B, S // tm, N // tn),
        in_specs=[pl.BlockSpec((None, tm, D), lambda b, s, n: (b, s, 0)),
                  pl.BlockSpec((1, D), lambda b, s, n: (0, 0)), vec, vec,
                  pl.BlockSpec((D, tn), lambda b, s, n: (0, n)),
                  pl.BlockSpec((D, tn), lambda b, s, n: (0, n)),
                  pl.BlockSpec((1, tn), lambda b, s, n: (0, n)),
                  pl.BlockSpec((1, tn), lambda b, s, n: (0, n))],
        out_specs=pl.BlockSpec((None, tm, tn), lambda b, s, n: (b, s, n)),
        out_shape=jax.ShapeDtypeStruct((B, S, N), BF16),
        scratch_shapes=[pltpu.VMEM((tm, D), BF16)],
        compiler_params=_params(("arbitrary", "arbitrary", "arbitrary")),
        name="norm_glu_proj",
    )(x, gain, scale, shift, wa, wb, ba, bb)


def _conv_kernel(u_ref, x_ref, wdw_ref, bdw_ref, ng_ref, w2_ref, b2_ref, g1_ref, o_ref,
                 ext_scr, conv_scr, *, row_block):
    tm, D = conv_scr.shape
    s = pl.program_id(1)

    @pl.when(s == 0)
    def _():
        ext_scr[0:CONV_HALO, :] = jnp.zeros((CONV_HALO, D), F32)

    @pl.when(s > 0)
    def _():
        ext_scr[0:CONV_HALO, :] = ext_scr[tm:tm + CONV_HALO, :]

    ext_scr[CONV_HALO:CONV_HALO + tm, :] = u_ref[...].astype(F32)

    first = CONV_HALO - (CONV_WIDTH - 1)

    def strip(j, carry):
        cols = pl.ds(pl.multiple_of(j * LANES, LANES), LANES)
        for r0 in range(0, tm, row_block):
            acc = jnp.zeros((row_block, LANES), F32) + bdw_ref[:, cols]
            for k in range(CONV_WIDTH):
                acc = acc + ext_scr[r0 + first + k:r0 + first + k + row_block, cols] * wdw_ref[k:k + 1, cols]
            conv_scr[r0:r0 + row_block, cols] = acc
        return carry

    lax.fori_loop(0, D // LANES, strip, 0)

    v = conv_scr[...]
    ms = jnp.mean(v * v, axis=-1, keepdims=True)
    v = _silu(v * lax.rsqrt(ms + EPS) * ng_ref[...])
    y = jnp.dot(v.astype(BF16), w2_ref[...], preferred_element_type=F32) + b2_ref[...]
    o_ref[...] = x_ref[...] + g1_ref[...] * y


def _conv_tail(u, x, w_dw, b_dw, norm_g, w2, b2, gate, tm):
    B, S, D = x.shape
    row = pl.BlockSpec((1, D), lambda b, s: (0, 0))
    return pl.pallas_call(
        functools.partial(_conv_kernel, row_block=64),
        grid=(B, S // tm),
        in_specs=[pl.BlockSpec((None, tm, D), lambda b, s: (b, s, 0)),
                  pl.BlockSpec((None, tm, D), lambda b, s: (b, s, 0)),
                  pl.BlockSpec((CONV_WIDTH, D), lambda b, s: (0, 0)),
                  row, row,
                  pl.BlockSpec((D, D), lambda b, s: (0, 0)),
                  row,
                  pl.BlockSpec((None, 1, D), lambda b, s: (b, 0, 0))],
        out_specs=pl.BlockSpec((None, tm, D), lambda b, s: (b, s, 0)),
        out_shape=jax.ShapeDtypeStruct((B, S, D), F32),
        scratch_shapes=[pltpu.VMEM((tm + CONV_HALO, D), F32), pltpu.VMEM((tm, D), F32)],
        compiler_params=_params(("arbitrary", "arbitrary")),
        name="conv_tail",
    )(u, x, w_dw, b_dw, norm_g, w2, b2, gate)


def _gla_kernel(q_ref, k_ref, v_ref, r_ref, a_ref, wa2_ref, ba_ref, hg_ref, o_ref, st_scr, *, scale):
    @pl.when(pl.program_id(2) == 0)
    def _():
        st_scr[...] = jnp.zeros_like(st_scr)

    rows = q_ref.shape[0]
    z = jnp.dot(a_ref[...], wa2_ref[...], preferred_element_type=F32) + ba_ref[...]
    la = (jnp.minimum(z, 0.0) - jnp.log1p(jnp.exp(-jnp.abs(z)))) * (1.0 / GLA_TAU)

    pos = lax.broadcasted_iota(jnp.int32, la.shape, 0) & (GLA_CHUNK - 1)
    cum = la
    step = 1
    while step < GLA_CHUNK:
        cum = cum + jnp.where(pos >= step, pltpu.roll(cum, step, axis=0), 0.0)
        step *= 2

    ci = lax.broadcasted_iota(jnp.int32, (GLA_CHUNK, GLA_CHUNK), 0)
    cj = lax.broadcasted_iota(jnp.int32, (GLA_CHUNK, GLA_CHUNK), 1)
    causal = cj <= ci
    nt = (((1,), (1,)), ((), ()))
    tn = (((0,), (0,)), ((), ()))
    for c in range(rows // GLA_CHUNK):
        sl = slice(c * GLA_CHUNK, (c + 1) * GLA_CHUNK)
        cum_c = cum[sl]
        last = cum_c[GLA_CHUNK - 1:GLA_CHUNK, :]
        k_f = k_ref[sl, :].astype(F32)
        q_dec = (q_ref[sl, :].astype(F32) * scale * jnp.exp(cum_c)).astype(BF16)
        k_inv = (k_f * jnp.exp(-cum_c)).astype(BF16)
        k_end = (k_f * jnp.exp(last - cum_c)).astype(BF16)
        v_c = v_ref[sl, :]
        att = lax.dot_general(q_dec, k_inv, nt, preferred_element_type=F32)
        att = jnp.where(causal, att, 0.0)
        st = st_scr[...]
        o = (jnp.dot(att.astype(BF16), v_c, preferred_element_type=F32)
             + lax.dot_general(q_dec, st.astype(BF16), nt, preferred_element_type=F32))
        st_scr[...] = jnp.exp(last) * st + lax.dot_general(v_c, k_end, tn, preferred_element_type=F32)
        o = o * lax.rsqrt(jnp.mean(o * o, axis=-1, keepdims=True) + EPS) * hg_ref[...]
        o_ref[sl, :] = (_silu(r_ref[sl, :].astype(F32)) * o).astype(o_ref.dtype)


def _gla_core(proj, wa2, ba, head_g, rows):
    B, S, _ = proj.shape
    H = GLA_HEADS
    dk = wa2.shape[1]
    hk = dk // H
    hv = 2 * hk
    a_blk = (2 * dk + 2 * H * hv) // LANES
    return pl.pallas_call(
        functools.partial(_gla_kernel, scale=float(hk) ** -0.5),
        grid=(B, H, S // rows),
        in_specs=[pl.BlockSpec((None, rows, hk), lambda b, h, t: (b, t, h)),
                  pl.BlockSpec((None, rows, hk), lambda b, h, t: (b, t, H + h)),
                  pl.BlockSpec((None, rows, hv), lambda b, h, t: (b, t, H + h)),
                  pl.BlockSpec((None, rows, hv), lambda b, h, t: (b, t, 2 * H + h)),
                  pl.BlockSpec((None, rows, LANES), lambda b, h, t: (b, t, a_blk)),
                  pl.BlockSpec((LANES, hk), lambda b, h, t: (0, h)),
                  pl.BlockSpec((1, hk), lambda b, h, t: (0, h)),
                  pl.BlockSpec((None, 1, hv), lambda b, h, t: (h, 0, 0))],
        out_specs=pl.BlockSpec((None, rows, hv), lambda b, h, t: (b, t, h)),
        out_shape=jax.ShapeDtypeStruct((B, S, H * hv), BF16),
        scratch_shapes=[pltpu.VMEM((hv, hk), F32)],
        compiler_params=_params(("arbitrary", "arbitrary", "arbitrary")),
        name="gla_core",
    )(proj, proj, proj, proj, proj, wa2, ba, head_g)


def _out_proj_kernel(a_ref, x_ref, w_ref, g_ref, o_ref):
    y = jnp.dot(a_ref[...], w_ref[...], preferred_element_type=F32)
    o_ref[...] = x_ref[...] + g_ref[...] * y


def _out_proj(a, x, w, gate, tm):
    B, S, D = x.shape
    K = a.shape[2]
    return pl.pallas_call(
        _out_proj_kernel,
        grid=(B, S // tm),
        in_specs=[pl.BlockSpec((None, tm, K), lambda b, s: (b, s, 0)),
                  pl.BlockSpec((None, tm, D), lambda b, s: (b, s, 0)),
                  pl.BlockSpec((K, D), lambda b, s: (0, 0)),
                  pl.BlockSpec((None, 1, D), lambda b, s: (b, 0, 0))],
        out_specs=pl.BlockSpec((None, tm, D), lambda b, s: (b, s, 0)),
        out_shape=jax.ShapeDtypeStruct((B, S, D), F32),
        compiler_params=_params(("arbitrary", "arbitrary")),
        name="out_proj",
    )(a, x, w, gate)


def _route_kernel(x_ref, g_ref, sc_ref, sh_ref, wr_hi_ref, wr_lo_ref, br_ref, h_ref, route_ref):
    h = _norm_mod(x_ref[...], g_ref[...], sc_ref[...], sh_ref[...])
    h_ref[...] = h
    h_hi = h.astype(BF16)
    h_lo = (h - h_hi.astype(F32)).astype(BF16)
    logits = (jnp.dot(h_hi, wr_hi_ref[...], preferred_element_type=F32)
              + jnp.dot(h_lo, wr_hi_ref[...], preferred_element_type=F32)
              + jnp.dot(h_hi, wr_lo_ref[...], preferred_element_type=F32)) + br_ref[...]
    lane = lax.broadcasted_iota(jnp.int32, logits.shape, 1)
    neg = -jnp.inf
    gmask = lane < N_GROUPS
    gl = jnp.where(gmask, logits, neg)
    gm = jnp.max(gl, axis=-1, keepdims=True)
    gsum = jnp.sum(jnp.where(gmask, jnp.exp(logits - gm), 0.0), axis=-1, keepdims=True)
    g_val = 1.0 / gsum
    g_idx = jnp.min(jnp.where(gl == gm, lane, LANES), axis=-1, keepdims=True)
    lo = N_GROUPS + EXPERTS_PER_GROUP * g_idx
    emask = (lane >= lo) & (lane < lo + EXPERTS_PER_GROUP)
    el = jnp.where(emask, logits, neg)
    m1 = jnp.max(el, axis=-1, keepdims=True)
    i1 = jnp.min(jnp.where(el == m1, lane, LANES), axis=-1, keepdims=True)
    el2 = jnp.where(lane == i1, neg, el)
    m2 = jnp.max(el2, axis=-1, keepdims=True)
    i2 = jnp.min(jnp.where(el2 == m2, lane, LANES), axis=-1, keepdims=True)
    e2 = jnp.exp(m2 - m1)
    w1 = g_val / (1.0 + e2)
    w2 = g_val * e2 / (1.0 + e2)
    route = jnp.where(lane == 0, w1, 0.0)
    route = jnp.where(lane == 1, w2, route)
    route = jnp.where(lane == 2, (i1 - N_GROUPS).astype(F32), route)
    route = jnp.where(lane == 3, (i2 - N_GROUPS).astype(F32), route)
    route_ref[...] = route


def _norm_route(x, gain, scale, shift, wr_hi, wr_lo, br, tm):
    B, S, D = x.shape
    vec = pl.BlockSpec((None, 1, D), lambda b, s: (b, 0, 0))
    return pl.pallas_call(
        _route_kernel,
        grid=(B, S // tm),
        in_specs=[pl.BlockSpec((None, tm, D), lambda b, s: (b, s, 0)),
                  pl.BlockSpec((1, D), lambda b, s: (0, 0)), vec, vec,
                  pl.BlockSpec((D, LANES), lambda b, s: (0, 0)),
                  pl.BlockSpec((D, LANES), lambda b, s: (0, 0)),
                  pl.BlockSpec((1, LANES), lambda b, s: (0, 0))],
        out_specs=[pl.BlockSpec((None, tm, D), lambda b, s: (b, s, 0)),
                   pl.BlockSpec((None, tm, LANES), lambda b, s: (b, s, 0))],
        out_shape=[jax.ShapeDtypeStruct((B, S, D), F32),
                   jax.ShapeDtypeStruct((B, S, LANES), F32)],
        compiler_params=_params(("arbitrary", "arbitrary")),
        name="norm_route",
    )(x, gain, scale, shift, wr_hi, wr_lo, br)


def _expert_kernel(be_ref, src_ref, nused_ref, h_hbm, w13_ref, w2_ref, ys_ref,
                   xbuf, sem, w13b, w2b):
    i = pl.program_id(0)
    n_used = nused_ref[0]
    de = w2b.shape[0]

    def row_copy(block, r, slot):
        tok = src_ref[block * MOE_BLOCK + r]
        return pltpu.make_async_copy(h_hbm.at[pl.ds(tok, 1)], xbuf.at[slot, pl.ds(r, 1)], sem.at[slot])

    def issue(block, slot):
        def body(r, carry):
            row_copy(block, r, slot).start()
            return carry
        lax.fori_loop(0, MOE_BLOCK, body, 0)

    @pl.when((i == 0) & (n_used > 0))
    def _():
        issue(0, 0)

    @pl.when(i + 1 < n_used)
    def _():
        issue(i + 1, (i + 1) % 2)

    @pl.when(i < n_used)
    def _():
        slot = i % 2

        def wait_body(r, carry):
            row_copy(i, r, slot).wait()
            return carry
        lax.fori_loop(0, MOE_BLOCK, wait_body, 0)

        prev = be_ref[jnp.maximum(i - 1, 0)]

        @pl.when((i == 0) | (be_ref[i] != prev))
        def _():
            w13b[...] = w13_ref[...].astype(BF16)
            w2b[...] = w2_ref[...].astype(BF16)

        xb = xbuf[slot].astype(BF16)
        ag = jnp.dot(xb, w13b[...], preferred_element_type=F32)
        act = (_silu(ag[:, :de]) * ag[:, de:]).astype(BF16)
        ys_ref[...] = jnp.dot(act, w2b[...], preferred_element_type=F32)

    @pl.when(i >= n_used)
    def _():
        ys_ref[...] = jnp.zeros_like(ys_ref)


def _experts(h2d, w13, w2, block_expert, src_tok, n_used):
    T, D = h2d.shape
    n_blocks = block_expert.shape[0]
    de = w2.shape[1]
    grid_spec = pltpu.PrefetchScalarGridSpec(
        num_scalar_prefetch=3,
        grid=(n_blocks,),
        in_specs=[pl.BlockSpec(memory_space=pl.ANY),
                  pl.BlockSpec((None, D, 2 * de), lambda i, be, src, nu: (be[i], 0, 0)),
                  pl.BlockSpec((None, de, D), lambda i, be, src, nu: (be[i], 0, 0))],
        out_specs=pl.BlockSpec((MOE_BLOCK, D), lambda i, be, src, nu: (i, 0)),
        scratch_shapes=[pltpu.VMEM((2, MOE_BLOCK, D), F32),
                        pltpu.SemaphoreType.DMA((2,)),
                        pltpu.VMEM((D, 2 * de), BF16),
                        pltpu.VMEM((de, D), BF16)])
    return pl.pallas_call(
        _expert_kernel,
        grid_spec=grid_spec,
        out_shape=jax.ShapeDtypeStruct((n_blocks * MOE_BLOCK, D), F32),
        compiler_params=_params(("arbitrary",)),
        name="moe_experts",
    )(block_expert, src_tok, n_used, h2d, w13, w2)


def _combine_kernel(dest_ref, ys_hbm, route_ref, x_ref, g_ref, o_ref, ybuf, sem, *, tm):
    j = pl.program_id(0)
    nj = pl.num_programs(0)

    def row_copy(tile, a, slot):
        d = dest_ref[tile * (2 * tm) + a]
        return pltpu.make_async_copy(ys_hbm.at[pl.ds(d, 1)], ybuf.at[slot, a % 2, pl.ds(a // 2, 1)], sem.at[slot])

    def issue(tile, slot):
        def body(a, carry):
            row_copy(tile, a, slot).start()
            return carry
        lax.fori_loop(0, 2 * tm, body, 0)

    @pl.when(j == 0)
    def _():
        issue(0, 0)

    @pl.when(j + 1 < nj)
    def _():
        issue(j + 1, (j + 1) % 2)

    slot = j % 2

    def wait_body(a, carry):
        row_copy(j, a, slot).wait()
        return carry
    lax.fori_loop(0, 2 * tm, wait_body, 0)

    route = route_ref[...]
    w1 = route[:, 0:1]
    w2 = route[:, 1:2]
    moe = w1 * ybuf[slot, 0] + w2 * ybuf[slot, 1]
    o_ref[...] = x_ref[...] + g_ref[...] * moe


def _combine(ys, dest, route2d, x2d, gate, seq, tm):
    T, D = x2d.shape
    tiles_per_seq = seq // tm
    grid_spec = pltpu.PrefetchScalarGridSpec(
        num_scalar_prefetch=1,
        grid=(T // tm,),
        in_specs=[pl.BlockSpec(memory_space=pl.ANY),
                  pl.BlockSpec((tm, LANES), lambda j, d: (j, 0)),
                  pl.BlockSpec((tm, D), lambda j, d: (j, 0)),
                  pl.BlockSpec((None, 1, D), lambda j, d: (j // tiles_per_seq, 0, 0))],
        out_specs=pl.BlockSpec((tm, D), lambda j, d: (j, 0)),
        scratch_shapes=[pltpu.VMEM((2, 2, tm, D), F32), pltpu.SemaphoreType.DMA((2,))])
    return pl.pallas_call(
        functools.partial(_combine_kernel, tm=tm),
        grid_spec=grid_spec,
        out_shape=jax.ShapeDtypeStruct((T, D), F32),
        compiler_params=_params(("arbitrary",)),
        name="moe_combine",
    )(dest, ys, route2d, x2d, gate)


def _dispatch_plan(expert):
    T = expert.shape[0]
    A = 2 * T
    n_blocks = A // MOE_BLOCK + N_EXPERTS
    flat_e = expert.reshape(A)
    order = jnp.argsort(flat_e).astype(jnp.int32)
    se = flat_e[order]
    counts = jnp.sum((flat_e[:, None] == jnp.arange(N_EXPERTS, dtype=jnp.int32)[None, :]).astype(jnp.int32), axis=0)
    starts = jnp.cumsum(counts) - counts
    padded = (counts + MOE_BLOCK - 1) // MOE_BLOCK * MOE_BLOCK
    pad_ends = jnp.cumsum(padded)
    pad_starts = pad_ends - padded
    dest_sorted = pad_starts[se] + jnp.arange(A, dtype=jnp.int32) - starts[se]
    dest = jnp.zeros((A,), jnp.int32).at[order].set(dest_sorted)
    src_tok = jnp.zeros((n_blocks * MOE_BLOCK,), jnp.int32).at[dest_sorted].set(order // 2)
    block_start = jnp.arange(n_blocks, dtype=jnp.int32) * MOE_BLOCK
    block_expert = jnp.minimum(jnp.searchsorted(pad_ends, block_start, side="right"), N_EXPERTS - 1).astype(jnp.int32)
    n_used = (pad_ends[-1:] // MOE_BLOCK).astype(jnp.int32)
    return block_expert, src_tok, n_used, dest


def _final_norm_kernel(x_ref, g_ref, o_ref):
    x = x_ref[...]
    ms = jnp.mean(x * x, axis=-1, keepdims=True)
    o_ref[...] = x * lax.rsqrt(ms + EPS) * g_ref[...]


def _final_norm(x2d, gain, tm):
    T, D = x2d.shape
    return pl.pallas_call(
        _final_norm_kernel,
        grid=(T // tm,),
        in_specs=[pl.BlockSpec((tm, D), lambda i: (i, 0)), pl.BlockSpec((1, D), lambda i: (0, 0))],
        out_specs=pl.BlockSpec((tm, D), lambda i: (i, 0)),
        out_shape=jax.ShapeDtypeStruct((T, D), F32),
        compiler_params=_params(("arbitrary",)),
        name="final_norm",
    )(x2d, gain)


def _pick(n, pref):
    t = min(n, pref)
    while n % t:
        t //= 2
    return t


def kernel(x, c, ada_w, ada_b, mix_norm_g, ffn_norm_g, conv_w_pw1, conv_b_pw1, conv_w_dw, conv_b_dw, conv_norm_g, conv_w_pw2, conv_b_pw2, gla_w_in, gla_w_a2, gla_b_a, gla_head_g, gla_w_o, moe_w_group, moe_b_group, moe_w_expert, moe_b_expert, moe_w13, moe_w2, final_norm_g):
    B, S, D = x.shape
    L = ada_w.shape[0]
    T = B * S
    mod = _ada_mod(c, ada_w, ada_b)
    mod = mod.reshape(L, B, N_ADA, 1, D)

    dk = gla_w_a2.shape[2]
    for i in range(L):
        sh1, sc1, g1, sh2, sc2, g2 = [mod[i, :, m] for m in range(N_ADA)]
        j = i // 2
        mg = mix_norm_g[i].reshape(1, D)
        if i % 2 == 0:
            w1 = conv_w_pw1[j].astype(BF16)
            b1 = conv_b_pw1[j].reshape(1, 2 * D)
            u = _norm_glu_proj(x, mg, sc1, sh1, w1[:, :D], w1[:, D:], b1[:, :D], b1[:, D:],
                               tm=_pick(S, 1024), tn=_pick(D, 512))
            x = _conv_tail(u, x, conv_w_dw[j], conv_b_dw[j].reshape(1, D), conv_norm_g[j].reshape(1, D),
                           conv_w_pw2[j].astype(BF16), conv_b_pw2[j].reshape(1, D), g1, tm=_pick(S, 256))
        else:
            w_in = gla_w_in[j]
            n_in = w_in.shape[1]
            n_pad = n_in - GLA_GATE_RANK + LANES
            w_in = jnp.zeros((D, n_pad), F32).at[:, :n_in].set(w_in).astype(BF16)
            tn = n_pad // 7 if n_pad % (7 * LANES) == 0 else LANES
            proj = _norm_proj(x, mg, sc1, sh1, w_in, tm=_pick(S, 1024), tn=tn)
            wa2 = jnp.zeros((LANES, dk), F32).at[:GLA_GATE_RANK].set(gla_w_a2[j]).astype(BF16)
            gated = _gla_core(proj, wa2, gla_b_a[j].reshape(1, dk),
                              gla_head_g[j].reshape(GLA_HEADS, 1, -1), rows=_pick(S, 512))
            x = _out_proj(gated, x, gla_w_o[j].astype(BF16), g1, tm=_pick(S, 512))

        wr = jnp.zeros((D, LANES), F32).at[:, :N_GROUPS].set(moe_w_group[i])
        wr = wr.at[:, N_GROUPS:N_GROUPS + N_EXPERTS].set(moe_w_expert[i])
        wr_hi = wr.astype(BF16)
        wr_lo = (wr - wr_hi.astype(F32)).astype(BF16)
        br = jnp.zeros((1, LANES), F32).at[0, :N_GROUPS].set(moe_b_group[i])
        br = br.at[0, N_GROUPS:N_GROUPS + N_EXPERTS].set(moe_b_expert[i])
        h, route = _norm_route(x, ffn_norm_g[i].reshape(1, D), sc2, sh2, wr_hi, wr_lo, br, tm=_pick(S, 512))
        route2d = route.reshape(T, LANES)
        expert = route2d[:, 2:4].astype(jnp.int32)
        block_expert, src_tok, n_used, dest = _dispatch_plan(expert)
        ys = _experts(h.reshape(T, D), moe_w13[i], moe_w2[i], block_expert, src_tok, n_used)
        x = _combine(ys, dest, route2d, x.reshape(T, D), g2, S, tm=_pick(S, 256)).reshape(B, S, D)

    return _final_norm(x.reshape(T, D), final_norm_g.reshape(1, D), tm=_pick(T, 512)).reshape(B, S, D)
```

```python
import functools

import jax
import jax.numpy as jnp
from jax import lax
from jax.experimental import pallas as pl
from jax.experimental.pallas import tpu as pltpu

EPS = 1e-6
CONV_WIDTH = 31
CONV_HALO = 32
GLA_HEADS = 4
GLA_GATE_RANK = 16
GLA_TAU = 16.0
GLA_CHUNK = 64
N_GROUPS = 4
EXPERTS_PER_GROUP = 8
N_EXPERTS = N_GROUPS * EXPERTS_PER_GROUP
MOE_BLOCK = 128
MOE_UNIT = 16
UNITS_PER_BLOCK = MOE_BLOCK // MOE_UNIT
MOE_TILE = 512
N_ADA = 6
LANES = 128
VMEM_LIMIT = 56 * 1024 * 1024

F32 = jnp.float32
BF16 = jnp.bfloat16


def _params(sem):
    return pltpu.CompilerParams(dimension_semantics=sem, vmem_limit_bytes=VMEM_LIMIT)


def _norm_mod(x, gain, scale, shift):
    ms = jnp.mean(x * x, axis=-1, keepdims=True)
    return x * lax.rsqrt(ms + EPS) * gain * (1.0 + scale) + shift


def _silu(x):
    return x * jax.nn.sigmoid(x)


def _ada_kernel(c_ref, w_ref, b_ref, o_ref):
    cond = _silu(c_ref[...])
    o_ref[...] = jnp.dot(cond.astype(BF16), w_ref[...].astype(BF16),
                         preferred_element_type=F32) + b_ref[...]


def _ada_mod(c, ada_w, ada_b):
    L, D, N = ada_w.shape
    B = c.shape[0]
    rows = 8
    c_pad = jnp.zeros((rows, D), F32).at[:B].set(c)
    tn = 1024
    out = pl.pallas_call(
        _ada_kernel,
        grid=(L, N // tn),
        in_specs=[pl.BlockSpec((rows, D), lambda l, n: (0, 0)),
                  pl.BlockSpec((None, D, tn), lambda l, n: (l, 0, n)),
                  pl.BlockSpec((None, 1, tn), lambda l, n: (l, 0, n))],
        out_specs=pl.BlockSpec((None, rows, tn), lambda l, n: (l, 0, n)),
        out_shape=jax.ShapeDtypeStruct((L, rows, N), F32),
        compiler_params=_params(("arbitrary", "arbitrary")),
        name="ada_mod",
    )(c_pad, ada_w, ada_b.reshape(L, 1, N))
    return out[:, :B]


def _proj_kernel(x_ref, g_ref, sc_ref, sh_ref, w_ref, o_ref, h_scr):
    @pl.when(pl.program_id(2) == 0)
    def _():
        h_scr[...] = _norm_mod(x_ref[...], g_ref[...], sc_ref[...], sh_ref[...]).astype(BF16)
    o_ref[...] = jnp.dot(h_scr[...], w_ref[...], preferred_element_type=F32).astype(o_ref.dtype)


def _norm_proj(x, gain, scale, shift, w, tm, tn):
    B, S, D = x.shape
    N = w.shape[1]
    return pl.pallas_call(
        _proj_kernel,
        grid=(B, S // tm, N // tn),
        in_specs=[pl.BlockSpec((None, tm, D), lambda b, s, n: (b, s, 0)),
                  pl.BlockSpec((1, D), lambda b, s, n: (0, 0)),
                  pl.BlockSpec((None, 1, D), lambda b, s, n: (b, 0, 0)),
                  pl.BlockSpec((None, 1, D), lambda b, s, n: (b, 0, 0)),
                  pl.BlockSpec((D, tn), lambda b, s, n: (0, n))],
        out_specs=pl.BlockSpec((None, tm, tn), lambda b, s, n: (b, s, n)),
        out_shape=jax.ShapeDtypeStruct((B, S, N), BF16),
        scratch_shapes=[pltpu.VMEM((tm, D), BF16)],
        compiler_params=_params(("arbitrary", "arbitrary", "arbitrary")),
        name="norm_proj",
    )(x, gain, scale, shift, w)


def _glu_proj_kernel(x_ref, g_ref, sc_ref, sh_ref, wa_ref, wb_ref, ba_ref, bb_ref, o_ref, h_scr):
    @pl.when(pl.program_id(2) == 0)
    def _():
        h_scr[...] = _norm_mod(x_ref[...], g_ref[...], sc_ref[...], sh_ref[...]).astype(BF16)
    h = h_scr[...]
    a = jnp.dot(h, wa_ref[...], preferred_element_type=F32) + ba_ref[...]
    b = jnp.dot(h, wb_ref[...], preferred_element_type=F32) + bb_ref[...]
    o_ref[...] = (a * jax.nn.sigmoid(b)).astype(o_ref.dtype)


def _norm_glu_proj(x, gain, scale, shift, wa, wb, ba, bb, tm, tn):
    B, S, D = x.shape
    N = wa.shape[1]
    vec = pl.BlockSpec((None, 1, D), lambda b, s, n: (b, 0, 0))
    return pl.pallas_call(
        _glu_proj_kernel,
        grid=(B, S // tm, N // tn),
        in_specs=[pl.BlockSpec((None, tm, D), lambda b, s, n: (b, s, 0)),
                  pl.BlockSpec((1, D), lambda b, s, n: (0, 0)), vec, vec,
                  pl.BlockSpec((D, tn), lambda b, s, n: (0, n)),
                  pl.BlockSpec((D, tn), lambda b, s, n: (0, n)),
                  pl.BlockSpec((1, tn), lambda b, s, n: (0, n)),
                  pl.BlockSpec((1, tn), lambda b, s, n: (0, n))],
        out_specs=pl.BlockSpec((None, tm, tn), lambda b, s, n: (b, s, n)),
        out_shape=jax.ShapeDtypeStruct((B, S, N), BF16),
        scratch_shapes=[pltpu.VMEM((tm, D), BF16)],
        compiler_params=_params(("arbitrary", "arbitrary", "arbitrary")),
        name="norm_glu_proj",
    )(x, gain, scale, shift, wa, wb, ba, bb)


def _conv_kernel(u_ref, x_ref, wdw_ref, bdw_ref, ng_ref, w2_ref, b2_ref, g1_ref, o_ref,
                 ext_scr, conv_scr, *, row_block):
    tm, D = conv_scr.shape
    s = pl.program_id(1)

    @pl.when(s == 0)
    def _():
        ext_scr[0:CONV_HALO, :] = jnp.zeros((CONV_HALO, D), F32)

    @pl.when(s > 0)
    def _():
        ext_scr[0:CONV_HALO, :] = ext_scr[tm:tm + CONV_HALO, :]

    ext_scr[CONV_HALO:CONV_HALO + tm, :] = u_ref[...].astype(F32)

    first = CONV_HALO - (CONV_WIDTH - 1)

    def strip(j, carry):
        cols = pl.ds(pl.multiple_of(j * LANES, LANES), LANES)
        for r0 in range(0, tm, row_block):
            acc = jnp.zeros((row_block, LANES), F32) + bdw_ref[:, cols]
            for k in range(CONV_WIDTH):
                acc = acc + ext_scr[r0 + first + k:r0 + first + k + row_block, cols] * wdw_ref[k:k + 1, cols]
            conv_scr[r0:r0 + row_block, cols] = acc
        return carry

    lax.fori_loop(0, D // LANES, strip, 0)

    v = conv_scr[...]
    ms = jnp.mean(v * v, axis=-1, keepdims=True)
    v = _silu(v * lax.rsqrt(ms + EPS) * ng_ref[...])
    y = jnp.dot(v.astype(BF16), w2_ref[...], preferred_element_type=F32) + b2_ref[...]
    o_ref[...] = x_ref[...] + g1_ref[...] * y


def _conv_tail(u, x, w_dw, b_dw, norm_g, w2, b2, gate, tm):
    B, S, D = x.shape
    row = pl.BlockSpec((1, D), lambda b, s: (0, 0))
    return pl.pallas_call(
        functools.partial(_conv_kernel, row_block=64),
        grid=(B, S // tm),
        in_specs=[pl.BlockSpec((None, tm, D), lambda b, s: (b, s, 0)),
                  pl.BlockSpec((None, tm, D), lambda b, s: (b, s, 0)),
                  pl.BlockSpec((CONV_WIDTH, D), lambda b, s: (0, 0)),
                  row, row,
                  pl.BlockSpec((D, D), lambda b, s: (0, 0)),
                  row,
                  pl.BlockSpec((None, 1, D), lambda b, s: (b, 0, 0))],
        out_specs=pl.BlockSpec((None, tm, D), lambda b, s: (b, s, 0)),
        out_shape=jax.ShapeDtypeStruct((B, S, D), F32),
        scratch_shapes=[pltpu.VMEM((tm + CONV_HALO, D), F32), pltpu.VMEM((tm, D), F32)],
        compiler_params=_params(("arbitrary", "arbitrary")),
        name="conv_tail",
    )(u, x, w_dw, b_dw, norm_g, w2, b2, gate)


def _gla_kernel(q_ref, k_ref, v_ref, r_ref, a_ref, wa2_ref, ba_ref, hg_ref, o_ref, st_scr, *, scale):
    @pl.when(pl.program_id(2) == 0)
    def _():
        st_scr[...] = jnp.zeros_like(st_scr)

    rows = q_ref.shape[0]
    z = jnp.dot(a_ref[...], wa2_ref[...], preferred_element_type=F32) + ba_ref[...]
    la = (jnp.minimum(z, 0.0) - jnp.log1p(jnp.exp(-jnp.abs(z)))) * (1.0 / GLA_TAU)

    pos = lax.broadcasted_iota(jnp.int32, la.shape, 0) & (GLA_CHUNK - 1)
    cum = la
    step = 1
    while step < GLA_CHUNK:
        cum = cum + jnp.where(pos >= step, pltpu.roll(cum, step, axis=0), 0.0)
        step *= 2

    ci = lax.broadcasted_iota(jnp.int32, (GLA_CHUNK, GLA_CHUNK), 0)
    cj = lax.broadcasted_iota(jnp.int32, (GLA_CHUNK, GLA_CHUNK), 1)
    causal = cj <= ci
    nt = (((1,), (1,)), ((), ()))
    tn = (((0,), (0,)), ((), ()))
    for c in range(rows // GLA_CHUNK):
        sl = slice(c * GLA_CHUNK, (c + 1) * GLA_CHUNK)
        cum_c = cum[sl]
        last = cum_c[GLA_CHUNK - 1:GLA_CHUNK, :]
        k_f = k_ref[sl, :].astype(F32)
        q_dec = (q_ref[sl, :].astype(F32) * scale * jnp.exp(cum_c)).astype(BF16)
        k_inv = (k_f * jnp.exp(-cum_c)).astype(BF16)
        k_end = (k_f * jnp.exp(last - cum_c)).astype(BF16)
        v_c = v_ref[sl, :]
        att = lax.dot_general(q_dec, k_inv, nt, preferred_element_type=F32)
        att = jnp.where(causal, att, 0.0)
        st = st_scr[...]
        o = (jnp.dot(att.astype(BF16), v_c, preferred_element_type=F32)
             + lax.dot_general(q_dec, st.astype(BF16), nt, preferred_element_type=F32))
        st_scr[...] = jnp.exp(last) * st + lax.dot_general(v_c, k_end, tn, preferred_element_type=F32)
        o = o * lax.rsqrt(jnp.mean(o * o, axis=-1, keepdims=True) + EPS) * hg_ref[...]
        o_ref[sl, :] = (_silu(r_ref[sl, :].astype(F32)) * o).astype(o_ref.dtype)


def _gla_core(proj, wa2, ba, head_g, rows):
    B, S, _ = proj.shape
    H = GLA_HEADS
    dk = wa2.shape[1]
    hk = dk // H
    hv = 2 * hk
    a_blk = (2 * dk + 2 * H * hv) // LANES
    return pl.pallas_call(
        functools.partial(_gla_kernel, scale=float(hk) ** -0.5),
        grid=(B, H, S // rows),
        in_specs=[pl.BlockSpec((None, rows, hk), lambda b, h, t: (b, t, h)),
                  pl.BlockSpec((None, rows, hk), lambda b, h, t: (b, t, H + h)),
                  pl.BlockSpec((None, rows, hv), lambda b, h, t: (b, t, H + h)),
                  pl.BlockSpec((None, rows, hv), lambda b, h, t: (b, t, 2 * H + h)),
                  pl.BlockSpec((None, rows, LANES), lambda b, h, t: (b, t, a_blk)),
                  pl.BlockSpec((LANES, hk), lambda b, h, t: (0, h)),
                  pl.BlockSpec((1, hk), lambda b, h, t: (0, h)),
                  pl.BlockSpec((None, 1, hv), lambda b, h, t: (h, 0, 0))],
        out_specs=pl.BlockSpec((None, rows, hv), lambda b, h, t: (b, t, h)),
        out_shape=jax.ShapeDtypeStruct((B, S, H * hv), BF16),
        scratch_shapes=[pltpu.VMEM((hv, hk), F32)],
        compiler_params=_params(("arbitrary", "arbitrary", "arbitrary")),
        name="gla_core",
    )(proj, proj, proj, proj, proj, wa2, ba, head_g)


def _out_proj_kernel(a_ref, x_ref, w_ref, g_ref, o_ref):
    y = jnp.dot(a_ref[...], w_ref[...], preferred_element_type=F32)
    o_ref[...] = x_ref[...] + g_ref[...] * y


def _out_proj(a, x, w, gate, tm):
    B, S, D = x.shape
    K = a.shape[2]
    return pl.pallas_call(
        _out_proj_kernel,
        grid=(B, S // tm),
        in_specs=[pl.BlockSpec((None, tm, K), lambda b, s: (b, s, 0)),
                  pl.BlockSpec((None, tm, D), lambda b, s: (b, s, 0)),
                  pl.BlockSpec((K, D), lambda b, s: (0, 0)),
                  pl.BlockSpec((None, 1, D), lambda b, s: (b, 0, 0))],
        out_specs=pl.BlockSpec((None, tm, D), lambda b, s: (b, s, 0)),
        out_shape=jax.ShapeDtypeStruct((B, S, D), F32),
        compiler_params=_params(("arbitrary", "arbitrary")),
        name="out_proj",
    )(a, x, w, gate)


def _split3(w):
    hi = w.astype(BF16).astype(F32)
    lo = (w - hi).astype(BF16).astype(F32)
    lo2 = (w - hi - lo).astype(BF16).astype(F32)
    return hi, lo, lo2


def _lane_pick(v, k):
    return v[:, k:k + 1]


def _route_kernel(x_ref, g_ref, sc_ref, sh_ref, wr_hi_ref, wr_lo_ref, br_ref, xs_ref, info_ref, tab_ref):
    tm, D = x_ref.shape
    cap = xs_ref.shape[0]
    h = _norm_mod(x_ref[...], g_ref[...], sc_ref[...], sh_ref[...])
    h_hi = h.astype(BF16)
    h_lo = (h - h_hi.astype(F32)).astype(BF16)
    logits = (jnp.dot(h_hi, wr_hi_ref[...], preferred_element_type=F32)
              + jnp.dot(h_lo, wr_hi_ref[...], preferred_element_type=F32)
              + jnp.dot(h_hi, wr_lo_ref[...], preferred_element_type=F32)) + br_ref[...]
    lane = lax.broadcasted_iota(jnp.int32, logits.shape, 1)
    neg = -jnp.inf
    gmask = lane < N_GROUPS
    gl = jnp.where(gmask, logits, neg)
    gm = jnp.max(gl, axis=-1, keepdims=True)
    gsum = jnp.sum(jnp.where(gmask, jnp.exp(logits - gm), 0.0), axis=-1, keepdims=True)
    g_val = 1.0 / gsum
    g_idx = jnp.min(jnp.where(gl == gm, lane, LANES), axis=-1, keepdims=True)
    lo = N_GROUPS + EXPERTS_PER_GROUP * g_idx
    emask = (lane >= lo) & (lane < lo + EXPERTS_PER_GROUP)
    el = jnp.where(emask, logits, neg)
    m1 = jnp.max(el, axis=-1, keepdims=True)
    i1 = jnp.min(jnp.where(el == m1, lane, LANES), axis=-1, keepdims=True)
    el2 = jnp.where(lane == i1, neg, el)
    m2 = jnp.max(el2, axis=-1, keepdims=True)
    i2 = jnp.min(jnp.where(el2 == m2, lane, LANES), axis=-1, keepdims=True)
    e2 = jnp.exp(m2 - m1)
    w1 = g_val / (1.0 + e2)
    w2 = g_val * e2 / (1.0 + e2)
    oh1 = lane == (i1 - N_GROUPS)
    oh2 = lane == (i2 - N_GROUPS)
    oh = jnp.where(oh1, 1.0, jnp.where(oh2, 1.0, 0.0))
    ri = lax.broadcasted_iota(jnp.int32, (tm, tm), 0)
    ci = lax.broadcasted_iota(jnp.int32, (tm, tm), 1)
    before = jnp.where(ri > ci, 1.0, 0.0).astype(BF16)
    rank = jnp.dot(before, oh.astype(BF16), preferred_element_type=F32)
    count = jnp.sum(oh, axis=0, keepdims=True)
    units = jnp.floor((count + (MOE_UNIT - 1.0)) * (1.0 / MOE_UNIT))
    ui = lax.broadcasted_iota(jnp.int32, (LANES, LANES), 0)
    uj = lax.broadcasted_iota(jnp.int32, (LANES, LANES), 1)
    earlier = jnp.where(ui < uj, 1.0, 0.0).astype(BF16)
    first_unit = jnp.dot(jnp.broadcast_to(units, (8, LANES)).astype(BF16), earlier,
                         preferred_element_type=F32)[0:1]
    pos = rank + first_unit * MOE_UNIT
    slot1 = jnp.sum(jnp.where(oh1, pos, 0.0), axis=-1, keepdims=True)
    slot2 = jnp.sum(jnp.where(oh2, pos, 0.0), axis=-1, keepdims=True)
    info = jnp.where(lane == 0, w1, 0.0)
    info = jnp.where(lane == 1, w2, info)
    info = jnp.where(lane == 2, slot1, info)
    info = jnp.where(lane == 3, slot2, info)
    info_ref[...] = info
    row8 = lax.broadcasted_iota(jnp.int32, (8, LANES), 0)
    tab = jnp.where(row8 == 0, units, jnp.where(row8 == 1, first_unit, 0.0))
    tab_ref[...] = tab.astype(jnp.int32)

    info_t = info.T
    s1_row = info_t[2:3, :]
    s2_row = info_t[3:4, :]
    r_idx = lax.broadcasted_iota(jnp.int32, (cap, tm), 0).astype(F32)
    place = jnp.where(r_idx == s1_row, 1.0, jnp.where(r_idx == s2_row, 1.0, 0.0)).astype(BF16)
    h_bf = h.astype(BF16)
    cw = 512 if D % 512 == 0 else LANES
    for c0 in range(0, D, cw):
        xs_ref[:, c0:c0 + cw] = jnp.dot(place, h_bf[:, c0:c0 + cw], preferred_element_type=F32).astype(BF16)

    w1s = _split3(w1)
    w2s = _split3(w2)
    s1_hi = jnp.floor(slot1 * (1.0 / MOE_UNIT))
    s1_lo = slot1 - s1_hi * MOE_UNIT
    aux = jnp.zeros((tm, LANES), F32)
    for k, col in enumerate(w1s + w2s + (s1_hi, s1_lo)):
        aux = jnp.where(lane == k, col, aux)
    got = jnp.dot(place, aux.astype(BF16), preferred_element_type=F32)
    row_slot1 = _lane_pick(got, 6) * MOE_UNIT + _lane_pick(got, 7)
    r_col = lax.broadcasted_iota(jnp.int32, (cap, 1), 0).astype(F32)
    w_row = jnp.where(row_slot1 == r_col,
                      _lane_pick(got, 0) + _lane_pick(got, 1) + _lane_pick(got, 2),
                      _lane_pick(got, 3) + _lane_pick(got, 4) + _lane_pick(got, 5))
    lane_c = lax.broadcasted_iota(jnp.int32, (cap, LANES), 1)
    wcols = jnp.zeros((cap, LANES), F32)
    for k, col in enumerate(_split3(w_row)):
        wcols = jnp.where(lane_c == k, col, wcols)
    xs_ref[:, D:D + LANES] = wcols.astype(BF16)


def _tile_capacity(tm):
    cap = 2 * tm + N_EXPERTS * (MOE_UNIT - 1)
    return -(-cap // LANES) * LANES


def _norm_route(x, gain, scale, shift, wr_hi, wr_lo, br, tm):
    B, S, D = x.shape
    spt = S // tm
    nt = B * spt
    cap = _tile_capacity(tm)
    vec = pl.BlockSpec((None, 1, D), lambda b, s: (b, 0, 0))
    return pl.pallas_call(
        _route_kernel,
        grid=(B, spt),
        in_specs=[pl.BlockSpec((None, tm, D), lambda b, s: (b, s, 0)),
                  pl.BlockSpec((1, D), lambda b, s: (0, 0)), vec, vec,
                  pl.BlockSpec((D, LANES), lambda b, s: (0, 0)),
                  pl.BlockSpec((D, LANES), lambda b, s: (0, 0)),
                  pl.BlockSpec((1, LANES), lambda b, s: (0, 0))],
        out_specs=[pl.BlockSpec((None, cap, D + LANES), lambda b, s: (b * spt + s, 0, 0)),
                   pl.BlockSpec((None, tm, LANES), lambda b, s: (b, s, 0)),
                   pl.BlockSpec((None, 8, LANES), lambda b, s: (b * spt + s, 0, 0))],
        out_shape=[jax.ShapeDtypeStruct((nt, cap, D + LANES), BF16),
                   jax.ShapeDtypeStruct((B, S, LANES), F32),
                   jax.ShapeDtypeStruct((nt, 8, LANES), jnp.int32)],
        compiler_params=_params(("arbitrary", "arbitrary")),
        name="norm_route",
    )(x, gain, scale, shift, wr_hi, wr_lo, br)


def _expert_kernel(be_ref, src_ref, nused_ref, xs_hbm, w13_ref, w2_ref, ys_ref,
                   xbuf, sem, w13b, w2b):
    i = pl.program_id(0)
    n_used = nused_ref[0]
    de = w2b.shape[0]
    D = w13b.shape[0]

    def unit_copy(block, u, slot):
        src = pl.multiple_of(src_ref[block * UNITS_PER_BLOCK + u] * MOE_UNIT, MOE_UNIT)
        return pltpu.make_async_copy(xs_hbm.at[pl.ds(src, MOE_UNIT)],
                                     xbuf.at[slot, pl.ds(u * MOE_UNIT, MOE_UNIT)], sem.at[slot])

    def issue(block, slot):
        for u in range(UNITS_PER_BLOCK):
            unit_copy(block, u, slot).start()

    @pl.when((i == 0) & (n_used > 0))
    def _():
        issue(0, 0)

    @pl.when(i + 1 < n_used)
    def _():
        issue(i + 1, (i + 1) % 2)

    @pl.when(i < n_used)
    def _():
        slot = i % 2
        for u in range(UNITS_PER_BLOCK):
            unit_copy(i, u, slot).wait()

        prev = be_ref[jnp.maximum(i - 1, 0)]

        @pl.when((i == 0) | (be_ref[i] != prev))
        def _():
            w13b[...] = w13_ref[...].astype(BF16)
            w2b[...] = w2_ref[...].astype(BF16)

        xb = xbuf[slot, :, 0:D]
        wc = xbuf[slot, :, D:D + LANES].astype(F32)
        w_row = _lane_pick(wc, 0) + _lane_pick(wc, 1) + _lane_pick(wc, 2)
        ag = jnp.dot(xb, w13b[...], preferred_element_type=F32)
        act = (_silu(ag[:, :de]) * ag[:, de:]).astype(BF16)
        ys_ref[...] = (w_row * jnp.dot(act, w2b[...], preferred_element_type=F32)).astype(ys_ref.dtype)

    @pl.when(i >= n_used)
    def _():
        ys_ref[...] = jnp.zeros_like(ys_ref)


def _experts(xs2d, w13, w2, layer, block_expert, src_unit, n_used):
    D = xs2d.shape[1] - LANES
    n_blocks = block_expert.shape[0]
    de = w2.shape[2]
    grid_spec = pltpu.PrefetchScalarGridSpec(
        num_scalar_prefetch=3,
        grid=(n_blocks,),
        in_specs=[pl.BlockSpec(memory_space=pl.ANY),
                  pl.BlockSpec((None, None, D, 2 * de), lambda i, be, src, nu: (layer, be[i], 0, 0)),
                  pl.BlockSpec((None, None, de, D), lambda i, be, src, nu: (layer, be[i], 0, 0))],
        out_specs=pl.BlockSpec((MOE_BLOCK, D), lambda i, be, src, nu: (i, 0)),
        scratch_shapes=[pltpu.VMEM((2, MOE_BLOCK, D + LANES), BF16),
                        pltpu.SemaphoreType.DMA((2,)),
                        pltpu.VMEM((D, 2 * de), BF16),
                        pltpu.VMEM((de, D), BF16)])
    return pl.pallas_call(
        _expert_kernel,
        grid_spec=grid_spec,
        out_shape=jax.ShapeDtypeStruct((n_blocks * MOE_BLOCK, D), BF16),
        compiler_params=_params(("arbitrary",)),
        name="moe_experts",
    )(block_expert, src_unit, n_used, xs2d, w13, w2)


def _combine_kernel(yu_ref, used_ref, ys_hbm, info_ref, x_ref, g_ref, o_ref, ybuf, sem):
    j = pl.program_id(0)
    nj = pl.num_programs(0)
    tm = x_ref.shape[0]
    cap = ybuf.shape[1]
    n_units = cap // MOE_UNIT

    def unit_copy(tile, q, slot):
        src = pl.multiple_of(yu_ref[tile * n_units + q] * MOE_UNIT, MOE_UNIT)
        dst = pl.multiple_of(q * MOE_UNIT, MOE_UNIT)
        return pltpu.make_async_copy(ys_hbm.at[pl.ds(src, MOE_UNIT)], ybuf.at[slot, pl.ds(dst, MOE_UNIT)], sem.at[slot])

    def issue(tile, slot):
        def body(q, carry):
            unit_copy(tile, q, slot).start()
            return carry
        lax.fori_loop(0, used_ref[tile], body, 0)

    @pl.when(j == 0)
    def _():
        issue(0, 0)

    @pl.when(j + 1 < nj)
    def _():
        issue(j + 1, (j + 1) % 2)

    slot = j % 2
    used = used_ref[j]

    def wait_body(q, carry):
        unit_copy(j, q, slot).wait()
        return carry
    lax.fori_loop(0, used, wait_body, 0)

    def clear_body(q, carry):
        ybuf[slot, pl.ds(pl.multiple_of(q * MOE_UNIT, MOE_UNIT), MOE_UNIT), :] = jnp.zeros((MOE_UNIT, ybuf.shape[2]), ybuf.dtype)
        return carry
    lax.fori_loop(used, n_units, clear_body, 0)

    info = info_ref[...]
    slot1 = _lane_pick(info, 2)
    slot2 = _lane_pick(info, 3)
    r_idx = lax.broadcasted_iota(jnp.int32, (tm, cap), 1).astype(F32)
    pick = jnp.where(r_idx == slot1, 1.0, jnp.where(r_idx == slot2, 1.0, 0.0)).astype(BF16)
    moe = jnp.dot(pick, ybuf[slot], preferred_element_type=F32)
    o_ref[...] = x_ref[...] + g_ref[...] * moe


def _combine(ys, ys_unit, used, info2d, x2d, gate, seq, tm):
    T, D = x2d.shape
    tiles_per_seq = seq // tm
    cap = _tile_capacity(tm)
    grid_spec = pltpu.PrefetchScalarGridSpec(
        num_scalar_prefetch=2,
        grid=(T // tm,),
        in_specs=[pl.BlockSpec(memory_space=pl.ANY),
                  pl.BlockSpec((tm, LANES), lambda j, yu, us: (j, 0)),
                  pl.BlockSpec((tm, D), lambda j, yu, us: (j, 0)),
                  pl.BlockSpec((None, 1, D), lambda j, yu, us: (j // tiles_per_seq, 0, 0))],
        out_specs=pl.BlockSpec((tm, D), lambda j, yu, us: (j, 0)),
        scratch_shapes=[pltpu.VMEM((2, cap, D), BF16), pltpu.SemaphoreType.DMA((2,))])
    return pl.pallas_call(
        _combine_kernel,
        grid_spec=grid_spec,
        out_shape=jax.ShapeDtypeStruct((T, D), F32),
        compiler_params=_params(("arbitrary",)),
        name="moe_combine",
    )(ys_unit, used, ys, info2d, x2d, gate)


def _moe_tables(tab, cap, n_blocks):
    nt = tab.shape[0]
    qc = cap // MOE_UNIT
    i32 = jnp.int32
    units = tab[:, 0, :N_EXPERTS]
    first = tab[:, 1, :N_EXPERTS]
    cum = jnp.cumsum(units, axis=0)
    excl = cum - units
    ne = cum[-1]
    nb = (ne + UNITS_PER_BLOCK - 1) // UNITS_PER_BLOCK
    bend = jnp.cumsum(nb)
    bstart = bend - nb
    n_used = bend[-1:].astype(i32)
    jb = jnp.arange(n_blocks, dtype=i32)
    block_expert = jnp.minimum(jnp.sum((bend[None, :] <= jb[:, None]).astype(i32), axis=1), N_EXPERTS - 1)
    v = jnp.arange(n_blocks * UNITS_PER_BLOCK, dtype=i32)
    e_v = block_expert[v // UNITS_PER_BLOCK]
    local = v - UNITS_PER_BLOCK * bstart[e_v]
    valid = (v // UNITS_PER_BLOCK < n_used[0]) & (local < ne[e_v])
    cum_v = cum.T[e_v]
    t_v = jnp.minimum(jnp.sum((cum_v <= local[:, None]).astype(i32), axis=1), nt - 1)
    src_unit = t_v * qc + first[t_v, e_v] + local - excl[t_v, e_v]
    src_unit = jnp.where(valid, src_unit, 0).astype(i32)
    q = jnp.arange(qc, dtype=i32)
    gend = first + units
    used = gend[:, -1].astype(i32)
    e_q = jnp.minimum(jnp.sum((gend[:, None, :] <= q[None, :, None]).astype(i32), axis=2), N_EXPERTS - 1)
    take = lambda a: jnp.take_along_axis(a, e_q, axis=1)
    ys_unit = UNITS_PER_BLOCK * bstart[e_q] + take(excl) + q[None, :] - take(first)
    ys_unit = jnp.where(q[None, :] < used[:, None], ys_unit, 0).astype(i32)
    return block_expert.astype(i32), src_unit, n_used, ys_unit.reshape(-1), used


def _final_norm_kernel(x_ref, g_ref, o_ref):
    x = x_ref[...]
    ms = jnp.mean(x * x, axis=-1, keepdims=True)
    o_ref[...] = x * lax.rsqrt(ms + EPS) * g_ref[...]


def _final_norm(x2d, gain, tm):
    T, D = x2d.shape
    return pl.pallas_call(
        _final_norm_kernel,
        grid=(T // tm,),
        in_specs=[pl.BlockSpec((tm, D), lambda i: (i, 0)), pl.BlockSpec((1, D), lambda i: (0, 0))],
        out_specs=pl.BlockSpec((tm, D), lambda i: (i, 0)),
        out_shape=jax.ShapeDtypeStruct((T, D), F32),
        compiler_params=_params(("arbitrary",)),
        name="final_norm",
    )(x2d, gain)


def _pick(n, pref):
    t = min(n, pref)
    while n % t:
        t //= 2
    return t


def kernel(x, c, ada_w, ada_b, mix_norm_g, ffn_norm_g, conv_w_pw1, conv_b_pw1, conv_w_dw, conv_b_dw, conv_norm_g, conv_w_pw2, conv_b_pw2, gla_w_in, gla_w_a2, gla_b_a, gla_head_g, gla_w_o, moe_w_group, moe_b_group, moe_w_expert, moe_b_expert, moe_w13, moe_w2, final_norm_g):
    B, S, D = x.shape
    L = ada_w.shape[0]
    T = B * S
    mod = _ada_mod(c, ada_w, ada_b)
    mod = mod.reshape(L, B, N_ADA, 1, D)

    dk = gla_w_a2.shape[2]
    for i in range(L):
        sh1, sc1, g1, sh2, sc2, g2 = [mod[i, :, m] for m in range(N_ADA)]
        j = i // 2
        mg = mix_norm_g[i].reshape(1, D)
        if i % 2 == 0:
            w1 = conv_w_pw1[j].astype(BF16)
            b1 = conv_b_pw1[j].reshape(1, 2 * D)
            u = _norm_glu_proj(x, mg, sc1, sh1, w1[:, :D], w1[:, D:], b1[:, :D], b1[:, D:],
                               tm=_pick(S, 1024), tn=_pick(D, 512))
            x = _conv_tail(u, x, conv_w_dw[j], conv_b_dw[j].reshape(1, D), conv_norm_g[j].reshape(1, D),
                           conv_w_pw2[j].astype(BF16), conv_b_pw2[j].reshape(1, D), g1, tm=_pick(S, 256))
        else:
            w_in = gla_w_in[j]
            n_in = w_in.shape[1]
            n_pad = n_in - GLA_GATE_RANK + LANES
            w_in = jnp.zeros((D, n_pad), F32).at[:, :n_in].set(w_in).astype(BF16)
            tn = n_pad // 7 if n_pad % (7 * LANES) == 0 else LANES
            proj = _norm_proj(x, mg, sc1, sh1, w_in, tm=_pick(S, 1024), tn=tn)
            wa2 = jnp.zeros((LANES, dk), F32).at[:GLA_GATE_RANK].set(gla_w_a2[j]).astype(BF16)
            gated = _gla_core(proj, wa2, gla_b_a[j].reshape(1, dk),
                              gla_head_g[j].reshape(GLA_HEADS, 1, -1), rows=_pick(S, 512))
            x = _out_proj(gated, x, gla_w_o[j].astype(BF16), g1, tm=_pick(S, 512))

        wr = jnp.zeros((D, LANES), F32).at[:, :N_GROUPS].set(moe_w_group[i])
        wr = wr.at[:, N_GROUPS:N_GROUPS + N_EXPERTS].set(moe_w_expert[i])
        wr_hi = wr.astype(BF16)
        wr_lo = (wr - wr_hi.astype(F32)).astype(BF16)
        br = jnp.zeros((1, LANES), F32).at[0, :N_GROUPS].set(moe_b_group[i])
        br = br.at[0, N_GROUPS:N_GROUPS + N_EXPERTS].set(moe_b_expert[i])
        tm = _pick(S, MOE_TILE)
        cap = _tile_capacity(tm)
        xs, info, tab = _norm_route(x, ffn_norm_g[i].reshape(1, D), sc2, sh2, wr_hi, wr_lo, br, tm=tm)
        n_blocks = -(-(T // tm) * cap // MOE_BLOCK) + N_EXPERTS
        block_expert, src_unit, n_used, ys_unit, used = _moe_tables(tab, cap, n_blocks)
        ys = _experts(xs.reshape(-1, D + LANES), moe_w13, moe_w2, i, block_expert, src_unit, n_used)
        x = _combine(ys, ys_unit, used, info.reshape(T, LANES), x.reshape(T, D), g2, S, tm=tm).reshape(B, S, D)

    return _final_norm(x.reshape(T, D), final_norm_g.reshape(1, D), tm=_pick(T, 512)).reshape(B, S, D)
```

```python
import functools

import jax
import jax.numpy as jnp
from jax import lax
from jax.experimental import pallas as pl
from jax.experimental.pallas import tpu as pltpu

EPS = 1e-6
CONV_WIDTH = 31
GLA_HEADS = 4
GLA_GATE_RANK = 16
GLA_TAU = 16.0
GLA_CHUNK = 64
N_GROUPS = 4
EXPERTS_PER_GROUP = 8
N_EXPERTS = N_GROUPS * EXPERTS_PER_GROUP
MOE_BLOCK = 128
MOE_UNIT = 16
UNITS_PER_BLOCK = MOE_BLOCK // MOE_UNIT
MOE_TILE = 512
PROJ_TILE_N = 1280
N_ADA = 6
LANES = 128
SUBLANES = 8
VMEM_LIMIT = 56 * 1024 * 1024

F32 = jnp.float32
BF16 = jnp.bfloat16


def _params(sem):
    return pltpu.CompilerParams(dimension_semantics=sem, vmem_limit_bytes=VMEM_LIMIT)


def _norm_mod(x, gain, scale, shift):
    ms = jnp.mean(x * x, axis=-1, keepdims=True)
    return x * lax.rsqrt(ms + EPS) * gain * (1.0 + scale) + shift


def _silu(x):
    return x * jax.nn.sigmoid(x)


def _ada_kernel(c_ref, w_ref, b_ref, o_ref):
    cond = _silu(c_ref[...])
    o_ref[...] = jnp.dot(cond.astype(BF16), w_ref[...].astype(BF16),
                         preferred_element_type=F32) + b_ref[...]


def _ada_mod(c, ada_w, ada_b):
    L, D, N = ada_w.shape
    B = c.shape[0]
    rows = 8
    c_pad = jnp.zeros((rows, D), F32).at[:B].set(c)
    tn = 1024
    out = pl.pallas_call(
        _ada_kernel,
        grid=(L, N // tn),
        in_specs=[pl.BlockSpec((rows, D), lambda l, n: (0, 0)),
                  pl.BlockSpec((None, D, tn), lambda l, n: (l, 0, n)),
                  pl.BlockSpec((None, 1, tn), lambda l, n: (l, 0, n))],
        out_specs=pl.BlockSpec((None, rows, tn), lambda l, n: (l, 0, n)),
        out_shape=jax.ShapeDtypeStruct((L, rows, N), F32),
        compiler_params=_params(("arbitrary", "arbitrary")),
        name="ada_mod",
    )(c_pad, ada_w, ada_b.reshape(L, 1, N))
    return out[:, :B]


def _proj_kernel(x_ref, g_ref, sc_ref, sh_ref, w_ref, o_ref, h_scr):
    @pl.when(pl.program_id(2) == 0)
    def _():
        h_scr[...] = _norm_mod(x_ref[...], g_ref[...], sc_ref[...], sh_ref[...]).astype(BF16)
    o_ref[...] = jnp.dot(h_scr[...], w_ref[...], preferred_element_type=F32).astype(o_ref.dtype)


def _norm_proj(x, gain, scale, shift, w, tm, tn):
    B, S, D = x.shape
    N = w.shape[1]
    return pl.pallas_call(
        _proj_kernel,
        grid=(B, S // tm, N // tn),
        in_specs=[pl.BlockSpec((None, tm, D), lambda b, s, n: (b, s, 0)),
                  pl.BlockSpec((1, D), lambda b, s, n: (0, 0)),
                  pl.BlockSpec((None, 1, D), lambda b, s, n: (b, 0, 0)),
                  pl.BlockSpec((None, 1, D), lambda b, s, n: (b, 0, 0)),
                  pl.BlockSpec((D, tn), lambda b, s, n: (0, n))],
        out_specs=pl.BlockSpec((None, tm, tn), lambda b, s, n: (b, s, n)),
        out_shape=jax.ShapeDtypeStruct((B, S, N), BF16),
        scratch_shapes=[pltpu.VMEM((tm, D), BF16)],
        compiler_params=_params(("arbitrary", "arbitrary", "arbitrary")),
        name="norm_proj",
    )(x, gain, scale, shift, w)


def _glu_proj_kernel(x_ref, g_ref, sc_ref, sh_ref, wa_ref, wb_ref, ba_ref, bb_ref, o_ref, h_scr):
    @pl.when(pl.program_id(2) == 0)
    def _():
        h_scr[...] = _norm_mod(x_ref[...], g_ref[...], sc_ref[...], sh_ref[...]).astype(BF16)
    h = h_scr[...]
    a = jnp.dot(h, wa_ref[...], preferred_element_type=F32) + ba_ref[...]
    b = jnp.dot(h, wb_ref[...], preferred_element_type=F32) + bb_ref[...]
    o_ref[...] = (a * jax.nn.sigmoid(b)).astype(o_ref.dtype)


def _norm_glu_proj(x, gain, scale, shift, wa, wb, ba, bb, tm, tn):
    B, S, D = x.shape
    N = wa.shape[1]
    vec = pl.BlockSpec((None, 1, D), lambda b, s, n: (b, 0, 0))
    return pl.pallas_call(
        _glu_proj_kernel,
        grid=(B, S // tm, N // tn),
        in_specs=[pl.BlockSpec((None, tm, D), lambda b, s, n: (b, s, 0)),
                  pl.BlockSpec((1, D), lambda b, s, n: (0, 0)), vec, vec,
                  pl.BlockSpec((D, tn), lambda b, s, n: (0, n)),
                  pl.BlockSpec((D, tn), lambda b, s, n: (0, n)),
                  pl.BlockSpec((1, tn), lambda b, s, n: (0, n)),
                  pl.BlockSpec((1, tn), lambda b, s, n: (0, n))],
        out_specs=pl.BlockSpec((None, tm, tn), lambda b, s, n: (b, s, n)),
        out_shape=jax.ShapeDtypeStruct((B, S, N), BF16),
        scratch_shapes=[pltpu.VMEM((tm, D), BF16)],
        compiler_params=_params(("arbitrary", "arbitrary", "arbitrary")),
        name="norm_glu_proj",
    )(x, gain, scale, shift, wa, wb, ba, bb)


def _conv_kernel(u_ref, x_ref, wdw_ref, bdw_ref, ng_ref, w2_ref, b2_ref, g1_ref, o_ref,
                 uf_scr, p_scr, halo_scr, conv_scr):
    tm, D = conv_scr.shape
    G = tm // SUBLANES
    NW = CONV_WIDTH - 1

    @pl.when(pl.program_id(1) == 0)
    def _():
        halo_scr[...] = jnp.zeros_like(halo_scr)

    def grp(j):
        return slice(j * SUBLANES, (j + 1) * SUBLANES)

    n_strips = D // LANES
    for c in range(n_strips):
        uf_scr[c] = u_ref[:, c * LANES:(c + 1) * LANES].astype(F32)
    sub = lax.broadcasted_iota(jnp.int32, (SUBLANES, LANES), 0)

    def strip(c, carry):
        cols = pl.ds(pl.multiple_of(c * LANES, LANES), LANES)
        for j in range(G):
            p_scr[grp(NW + j), cols] = uf_scr[c, pl.ds(j, SUBLANES, stride=G), :]
        for b in range(NW):
            cur = p_scr[grp(b + G), cols]
            prev = halo_scr[grp(b), cols]
            p_scr[grp(b), cols] = pltpu.roll(jnp.where(sub == SUBLANES - 1, prev, cur), 1, axis=0)
        halo_scr[:, cols] = p_scr[G * SUBLANES:(G + NW) * SUBLANES, cols]
        w = [wdw_ref[k:k + 1, cols] for k in range(CONV_WIDTH)]
        bias = bdw_ref[:, cols]
        for j in range(G):
            acc = p_scr[grp(j), cols] * w[0] + bias
            for k in range(1, CONV_WIDTH):
                acc = acc + p_scr[grp(j + k), cols] * w[k]
            conv_scr[grp(j), cols] = acc
        return carry

    lax.fori_loop(0, n_strips, strip, 0)

    v = conv_scr[...]
    ms = jnp.mean(v * v, axis=-1, keepdims=True)
    v = _silu(v * lax.rsqrt(ms + EPS) * ng_ref[...])
    y = jnp.dot(v.astype(BF16), w2_ref[...], preferred_element_type=F32) + b2_ref[...]
    for c in range(n_strips):
        uf_scr[c] = y[:, c * LANES:(c + 1) * LANES]

    def unstride(c, carry):
        cols = pl.ds(pl.multiple_of(c * LANES, LANES), LANES)
        g1 = g1_ref[:, cols]
        for m in range(G):
            t0 = m * SUBLANES
            start = (t0 % G) * SUBLANES + t0 // G
            o_ref[grp(m), cols] = x_ref[grp(m), cols] + g1 * uf_scr[c, pl.ds(start, SUBLANES, stride=SUBLANES), :]
        return carry

    lax.fori_loop(0, n_strips, unstride, 0)


def _conv_tail(u, x, w_dw, b_dw, norm_g, w2, b2, gate, tm):
    B, S, D = x.shape
    G = tm // SUBLANES
    assert G % SUBLANES == 0 and G >= CONV_WIDTH - 1
    row = pl.BlockSpec((1, D), lambda b, s: (0, 0))
    return pl.pallas_call(
        _conv_kernel,
        grid=(B, S // tm),
        in_specs=[pl.BlockSpec((None, tm, D), lambda b, s: (b, s, 0)),
                  pl.BlockSpec((None, tm, D), lambda b, s: (b, s, 0)),
                  pl.BlockSpec((CONV_WIDTH, D), lambda b, s: (0, 0)),
                  row, row,
                  pl.BlockSpec((D, D), lambda b, s: (0, 0)),
                  row,
                  pl.BlockSpec((None, 1, D), lambda b, s: (b, 0, 0))],
        out_specs=pl.BlockSpec((None, tm, D), lambda b, s: (b, s, 0)),
        out_shape=jax.ShapeDtypeStruct((B, S, D), F32),
        scratch_shapes=[pltpu.VMEM((D // LANES, tm, LANES), F32),
                        pltpu.VMEM((tm + (CONV_WIDTH - 1) * SUBLANES, D), F32),
                        pltpu.VMEM(((CONV_WIDTH - 1) * SUBLANES, D), F32),
                        pltpu.VMEM((tm, D), F32)],
        compiler_params=_params(("arbitrary", "arbitrary")),
        name="conv_tail",
    )(u, x, w_dw, b_dw, norm_g, w2, b2, gate)


def _gla_kernel(q_ref, k_ref, v_ref, r_ref, a_ref, wa2_ref, ba_ref, hg_ref, o_ref, st_scr, *, scale):
    @pl.when(pl.program_id(2) == 0)
    def _():
        st_scr[...] = jnp.zeros_like(st_scr)

    rows = q_ref.shape[0]
    z = jnp.dot(a_ref[...], wa2_ref[...], preferred_element_type=F32) + ba_ref[...]
    la = (jnp.minimum(z, 0.0) - jnp.log1p(jnp.exp(-jnp.abs(z)))) * (1.0 / GLA_TAU)

    pos = lax.broadcasted_iota(jnp.int32, la.shape, 0) & (GLA_CHUNK - 1)
    cum = la
    step = 1
    while step < GLA_CHUNK:
        cum = cum + jnp.where(pos >= step, pltpu.roll(cum, step, axis=0), 0.0)
        step *= 2

    ci = lax.broadcasted_iota(jnp.int32, (GLA_CHUNK, GLA_CHUNK), 0)
    cj = lax.broadcasted_iota(jnp.int32, (GLA_CHUNK, GLA_CHUNK), 1)
    causal = cj <= ci
    nt = (((1,), (1,)), ((), ()))
    tn = (((0,), (0,)), ((), ()))
    for c in range(rows // GLA_CHUNK):
        sl = slice(c * GLA_CHUNK, (c + 1) * GLA_CHUNK)
        cum_c = cum[sl]
        last = cum_c[GLA_CHUNK - 1:GLA_CHUNK, :]
        k_f = k_ref[sl, :].astype(F32)
        q_dec = (q_ref[sl, :].astype(F32) * scale * jnp.exp(cum_c)).astype(BF16)
        k_inv = (k_f * jnp.exp(-cum_c)).astype(BF16)
        k_end = (k_f * jnp.exp(last - cum_c)).astype(BF16)
        v_c = v_ref[sl, :]
        att = lax.dot_general(q_dec, k_inv, nt, preferred_element_type=F32)
        att = jnp.where(causal, att, 0.0)
        st = st_scr[...]
        o = (jnp.dot(att.astype(BF16), v_c, preferred_element_type=F32)
             + lax.dot_general(q_dec, st.astype(BF16), nt, preferred_element_type=F32))
        st_scr[...] = jnp.exp(last) * st + lax.dot_general(v_c, k_end, tn, preferred_element_type=F32)
        o = o * lax.rsqrt(jnp.mean(o * o, axis=-1, keepdims=True) + EPS) * hg_ref[...]
        o_ref[sl, :] = (_silu(r_ref[sl, :].astype(F32)) * o).astype(o_ref.dtype)


def _gla_core(proj, wa2, ba, head_g, rows):
    B, S, _ = proj.shape
    H = GLA_HEADS
    dk = wa2.shape[1]
    hk = dk // H
    hv = 2 * hk
    a_blk = (2 * dk + 2 * H * hv) // LANES
    return pl.pallas_call(
        functools.partial(_gla_kernel, scale=float(hk) ** -0.5),
        grid=(B, H, S // rows),
        in_specs=[pl.BlockSpec((None, rows, hk), lambda b, h, t: (b, t, h)),
                  pl.BlockSpec((None, rows, hk), lambda b, h, t: (b, t, H + h)),
                  pl.BlockSpec((None, rows, hv), lambda b, h, t: (b, t, H + h)),
                  pl.BlockSpec((None, rows, hv), lambda b, h, t: (b, t, 2 * H + h)),
                  pl.BlockSpec((None, rows, LANES), lambda b, h, t: (b, t, a_blk)),
                  pl.BlockSpec((LANES, hk), lambda b, h, t: (0, h)),
                  pl.BlockSpec((1, hk), lambda b, h, t: (0, h)),
                  pl.BlockSpec((None, 1, hv), lambda b, h, t: (h, 0, 0))],
        out_specs=pl.BlockSpec((None, rows, hv), lambda b, h, t: (b, t, h)),
        out_shape=jax.ShapeDtypeStruct((B, S, H * hv), BF16),
        scratch_shapes=[pltpu.VMEM((hv, hk), F32)],
        compiler_params=_params(("arbitrary", "arbitrary", "arbitrary")),
        name="gla_core",
    )(proj, proj, proj, proj, proj, wa2, ba, head_g)


def _out_proj_kernel(a_ref, x_ref, w_ref, g_ref, o_ref):
    y = jnp.dot(a_ref[...], w_ref[...], preferred_element_type=F32)
    o_ref[...] = x_ref[...] + g_ref[...] * y


def _out_proj(a, x, w, gate, tm):
    B, S, D = x.shape
    K = a.shape[2]
    return pl.pallas_call(
        _out_proj_kernel,
        grid=(B, S // tm),
        in_specs=[pl.BlockSpec((None, tm, K), lambda b, s: (b, s, 0)),
                  pl.BlockSpec((None, tm, D), lambda b, s: (b, s, 0)),
                  pl.BlockSpec((K, D), lambda b, s: (0, 0)),
                  pl.BlockSpec((None, 1, D), lambda b, s: (b, 0, 0))],
        out_specs=pl.BlockSpec((None, tm, D), lambda b, s: (b, s, 0)),
        out_shape=jax.ShapeDtypeStruct((B, S, D), F32),
        compiler_params=_params(("arbitrary", "arbitrary")),
        name="out_proj",
    )(a, x, w, gate)


def _split3(w):
    hi = w.astype(BF16).astype(F32)
    lo = (w - hi).astype(BF16).astype(F32)
    lo2 = (w - hi - lo).astype(BF16).astype(F32)
    return hi, lo, lo2


def _lane_pick(v, k):
    return v[:, k:k + 1]


def _route_kernel(x_ref, g_ref, sc_ref, sh_ref, wr_hi_ref, wr_lo_ref, br_ref, xs_ref, info_ref, tab_ref):
    tm, D = x_ref.shape
    cap = xs_ref.shape[0]
    h = _norm_mod(x_ref[...], g_ref[...], sc_ref[...], sh_ref[...])
    h_hi = h.astype(BF16)
    h_lo = (h - h_hi.astype(F32)).astype(BF16)
    logits = (jnp.dot(h_hi, wr_hi_ref[...], preferred_element_type=F32)
              + jnp.dot(h_lo, wr_hi_ref[...], preferred_element_type=F32)
              + jnp.dot(h_hi, wr_lo_ref[...], preferred_element_type=F32)) + br_ref[...]
    lane = lax.broadcasted_iota(jnp.int32, logits.shape, 1)
    neg = -jnp.inf
    gmask = lane < N_GROUPS
    gl = jnp.where(gmask, logits, neg)
    gm = jnp.max(gl, axis=-1, keepdims=True)
    gsum = jnp.sum(jnp.where(gmask, jnp.exp(logits - gm), 0.0), axis=-1, keepdims=True)
    g_val = 1.0 / gsum
    g_idx = jnp.min(jnp.where(gl == gm, lane, LANES), axis=-1, keepdims=True)
    lo = N_GROUPS + EXPERTS_PER_GROUP * g_idx
    emask = (lane >= lo) & (lane < lo + EXPERTS_PER_GROUP)
    el = jnp.where(emask, logits, neg)
    m1 = jnp.max(el, axis=-1, keepdims=True)
    i1 = jnp.min(jnp.where(el == m1, lane, LANES), axis=-1, keepdims=True)
    el2 = jnp.where(lane == i1, neg, el)
    m2 = jnp.max(el2, axis=-1, keepdims=True)
    i2 = jnp.min(jnp.where(el2 == m2, lane, LANES), axis=-1, keepdims=True)
    e2 = jnp.exp(m2 - m1)
    w1 = g_val / (1.0 + e2)
    w2 = g_val * e2 / (1.0 + e2)
    oh1 = lane == (i1 - N_GROUPS)
    oh2 = lane == (i2 - N_GROUPS)
    oh = jnp.where(oh1, 1.0, jnp.where(oh2, 1.0, 0.0))
    ri = lax.broadcasted_iota(jnp.int32, (tm, tm), 0)
    ci = lax.broadcasted_iota(jnp.int32, (tm, tm), 1)
    before = jnp.where(ri > ci, 1.0, 0.0).astype(BF16)
    rank = jnp.dot(before, oh.astype(BF16), preferred_element_type=F32)
    count = jnp.sum(oh, axis=0, keepdims=True)
    units = jnp.floor((count + (MOE_UNIT - 1.0)) * (1.0 / MOE_UNIT))
    ui = lax.broadcasted_iota(jnp.int32, (LANES, LANES), 0)
    uj = lax.broadcasted_iota(jnp.int32, (LANES, LANES), 1)
    earlier = jnp.where(ui < uj, 1.0, 0.0).astype(BF16)
    first_unit = jnp.dot(jnp.broadcast_to(units, (8, LANES)).astype(BF16), earlier,
                         preferred_element_type=F32)[0:1]
    pos = rank + first_unit * MOE_UNIT
    slot1 = jnp.sum(jnp.where(oh1, pos, 0.0), axis=-1, keepdims=True)
    slot2 = jnp.sum(jnp.where(oh2, pos, 0.0), axis=-1, keepdims=True)
    info = jnp.where(lane == 0, w1, 0.0)
    info = jnp.where(lane == 1, w2, info)
    info = jnp.where(lane == 2, slot1, info)
    info = jnp.where(lane == 3, slot2, info)
    info_ref[...] = info
    row8 = lax.broadcasted_iota(jnp.int32, (8, LANES), 0)
    tab = jnp.where(row8 == 0, units, jnp.where(row8 == 1, first_unit, 0.0))
    tab_ref[...] = tab.astype(jnp.int32)

    info_t = info.T
    s1_row = info_t[2:3, :]
    s2_row = info_t[3:4, :]
    r_idx = lax.broadcasted_iota(jnp.int32, (cap, tm), 0).astype(F32)
    place = jnp.where(r_idx == s1_row, 1.0, jnp.where(r_idx == s2_row, 1.0, 0.0)).astype(BF16)
    h_bf = h.astype(BF16)
    cw = 512 if D % 512 == 0 else LANES
    for c0 in range(0, D, cw):
        xs_ref[:, c0:c0 + cw] = jnp.dot(place, h_bf[:, c0:c0 + cw], preferred_element_type=F32).astype(BF16)

    w1s = _split3(w1)
    w2s = _split3(w2)
    s1_hi = jnp.floor(slot1 * (1.0 / MOE_UNIT))
    s1_lo = slot1 - s1_hi * MOE_UNIT
    aux = jnp.zeros((tm, LANES), F32)
    for k, col in enumerate(w1s + w2s + (s1_hi, s1_lo)):
        aux = jnp.where(lane == k, col, aux)
    got = jnp.dot(place, aux.astype(BF16), preferred_element_type=F32)
    row_slot1 = _lane_pick(got, 6) * MOE_UNIT + _lane_pick(got, 7)
    r_col = lax.broadcasted_iota(jnp.int32, (cap, 1), 0).astype(F32)
    w_row = jnp.where(row_slot1 == r_col,
                      _lane_pick(got, 0) + _lane_pick(got, 1) + _lane_pick(got, 2),
                      _lane_pick(got, 3) + _lane_pick(got, 4) + _lane_pick(got, 5))
    lane_c = lax.broadcasted_iota(jnp.int32, (cap, LANES), 1)
    wcols = jnp.zeros((cap, LANES), F32)
    for k, col in enumerate(_split3(w_row)):
        wcols = jnp.where(lane_c == k, col, wcols)
    xs_ref[:, D:D + LANES] = wcols.astype(BF16)


def _tile_capacity(tm):
    cap = 2 * tm + N_EXPERTS * (MOE_UNIT - 1)
    return -(-cap // LANES) * LANES


def _norm_route(x, gain, scale, shift, wr_hi, wr_lo, br, tm):
    B, S, D = x.shape
    spt = S // tm
    nt = B * spt
    cap = _tile_capacity(tm)
    vec = pl.BlockSpec((None, 1, D), lambda b, s: (b, 0, 0))
    return pl.pallas_call(
        _route_kernel,
        grid=(B, spt),
        in_specs=[pl.BlockSpec((None, tm, D), lambda b, s: (b, s, 0)),
                  pl.BlockSpec((1, D), lambda b, s: (0, 0)), vec, vec,
                  pl.BlockSpec((D, LANES), lambda b, s: (0, 0)),
                  pl.BlockSpec((D, LANES), lambda b, s: (0, 0)),
                  pl.BlockSpec((1, LANES), lambda b, s: (0, 0))],
        out_specs=[pl.BlockSpec((None, cap, D + LANES), lambda b, s: (b * spt + s, 0, 0)),
                   pl.BlockSpec((None, tm, LANES), lambda b, s: (b, s, 0)),
                   pl.BlockSpec((None, 8, LANES), lambda b, s: (b * spt + s, 0, 0))],
        out_shape=[jax.ShapeDtypeStruct((nt, cap, D + LANES), BF16),
                   jax.ShapeDtypeStruct((B, S, LANES), F32),
                   jax.ShapeDtypeStruct((nt, 8, LANES), jnp.int32)],
        compiler_params=_params(("arbitrary", "arbitrary")),
        name="norm_route",
    )(x, gain, scale, shift, wr_hi, wr_lo, br)


def _expert_kernel(be_ref, src_ref, wplan_ref, nused_ref, xs_hbm, w13_hbm, w2_hbm, ys_ref,
                   xbuf, sem, w13f, w2f, wsem, w13b, w2b, *, layer):
    i = pl.program_id(0)
    nb = pl.num_programs(0)
    n_used = nused_ref[0]
    de = w2b.shape[0]
    D = w13b.shape[0]

    def weight_copies(expert, slot):
        return (pltpu.make_async_copy(w13_hbm.at[layer, expert], w13f.at[slot], wsem.at[0, slot]),
                pltpu.make_async_copy(w2_hbm.at[layer, expert], w2f.at[slot], wsem.at[1, slot]))

    def unit_copy(block, u, slot):
        src = pl.multiple_of(src_ref[block * UNITS_PER_BLOCK + u] * MOE_UNIT, MOE_UNIT)
        return pltpu.make_async_copy(xs_hbm.at[pl.ds(src, MOE_UNIT)],
                                     xbuf.at[slot, pl.ds(u * MOE_UNIT, MOE_UNIT)], sem.at[slot])

    def issue(block, slot):
        for u in range(UNITS_PER_BLOCK):
            unit_copy(block, u, slot).start()

    @pl.when((i == 0) & (n_used > 0))
    def _():
        issue(0, 0)

    @pl.when(i + 1 < n_used)
    def _():
        issue(i + 1, (i + 1) % 2)

    @pl.when(i < n_used)
    def _():
        slot = i % 2
        for u in range(UNITS_PER_BLOCK):
            unit_copy(i, u, slot).wait()

        @pl.when(wplan_ref[i] == 1)
        def _():
            expert = be_ref[i]
            wslot = wplan_ref[nb + i]
            nxt = wplan_ref[2 * nb + i]

            @pl.when(i == 0)
            def _():
                for cp in weight_copies(expert, wslot):
                    cp.start()

            for cp in weight_copies(expert, wslot):
                cp.wait()
            w13b[...] = w13f[wslot].astype(BF16)
            w2b[...] = w2f[wslot].astype(BF16)

            @pl.when(nxt >= 0)
            def _():
                for cp in weight_copies(nxt, 1 - wslot):
                    cp.start()

        xb = xbuf[slot, :, 0:D]
        wc = xbuf[slot, :, D:D + LANES].astype(F32)
        w_row = _lane_pick(wc, 0) + _lane_pick(wc, 1) + _lane_pick(wc, 2)
        ag = jnp.dot(xb, w13b[...], preferred_element_type=F32)
        act = (_silu(ag[:, :de]) * ag[:, de:]).astype(BF16)
        ys_ref[...] = (w_row * jnp.dot(act, w2b[...], preferred_element_type=F32)).astype(ys_ref.dtype)

    @pl.when(i >= n_used)
    def _():
        ys_ref[...] = jnp.zeros_like(ys_ref)


def _experts(xs2d, w13, w2, layer, block_expert, src_unit, wplan, n_used):
    D = xs2d.shape[1] - LANES
    n_blocks = block_expert.shape[0]
    de = w2.shape[2]
    grid_spec = pltpu.PrefetchScalarGridSpec(
        num_scalar_prefetch=4,
        grid=(n_blocks,),
        in_specs=[pl.BlockSpec(memory_space=pl.ANY),
                  pl.BlockSpec(memory_space=pl.ANY),
                  pl.BlockSpec(memory_space=pl.ANY)],
        out_specs=pl.BlockSpec((MOE_BLOCK, D), lambda i, be, src, wp, nu: (i, 0)),
        scratch_shapes=[pltpu.VMEM((2, MOE_BLOCK, D + LANES), BF16),
                        pltpu.SemaphoreType.DMA((2,)),
                        pltpu.VMEM((2, D, 2 * de), F32),
                        pltpu.VMEM((2, de, D), F32),
                        pltpu.SemaphoreType.DMA((2, 2)),
                        pltpu.VMEM((D, 2 * de), BF16),
                        pltpu.VMEM((de, D), BF16)])
    return pl.pallas_call(
        functools.partial(_expert_kernel, layer=layer),
        grid_spec=grid_spec,
        out_shape=jax.ShapeDtypeStruct((n_blocks * MOE_BLOCK, D), BF16),
        compiler_params=_params(("arbitrary",)),
        name="moe_experts",
    )(block_expert, src_unit, wplan, n_used, xs2d, w13, w2)


def _combine_kernel(yu_ref, used_ref, ys_hbm, info_ref, x_ref, g_ref, fg_ref, o_ref, ybuf, sem, *, final_norm):
    j = pl.program_id(0)
    nj = pl.num_programs(0)
    tm = x_ref.shape[0]
    cap = ybuf.shape[1]
    n_units = cap // MOE_UNIT

    def unit_copy(tile, q, slot):
        src = pl.multiple_of(yu_ref[tile * n_units + q] * MOE_UNIT, MOE_UNIT)
        dst = pl.multiple_of(q * MOE_UNIT, MOE_UNIT)
        return pltpu.make_async_copy(ys_hbm.at[pl.ds(src, MOE_UNIT)], ybuf.at[slot, pl.ds(dst, MOE_UNIT)], sem.at[slot])

    def issue(tile, slot):
        def body(q, carry):
            unit_copy(tile, q, slot).start()
            return carry
        lax.fori_loop(0, used_ref[tile], body, 0)

    @pl.when(j == 0)
    def _():
        issue(0, 0)

    @pl.when(j + 1 < nj)
    def _():
        issue(j + 1, (j + 1) % 2)

    slot = j % 2
    used = used_ref[j]

    def wait_body(q, carry):
        unit_copy(j, q, slot).wait()
        return carry
    lax.fori_loop(0, used, wait_body, 0)

    def clear_body(q, carry):
        ybuf[slot, pl.ds(pl.multiple_of(q * MOE_UNIT, MOE_UNIT), MOE_UNIT), :] = jnp.zeros((MOE_UNIT, ybuf.shape[2]), ybuf.dtype)
        return carry
    lax.fori_loop(used, n_units, clear_body, 0)

    info = info_ref[...]
    slot1 = _lane_pick(info, 2)
    slot2 = _lane_pick(info, 3)
    r_idx = lax.broadcasted_iota(jnp.int32, (tm, cap), 1).astype(F32)
    pick = jnp.where(r_idx == slot1, 1.0, jnp.where(r_idx == slot2, 1.0, 0.0)).astype(BF16)
    moe = jnp.dot(pick, ybuf[slot], preferred_element_type=F32)
    out = x_ref[...] + g_ref[...] * moe
    if final_norm:
        out = out * lax.rsqrt(jnp.mean(out * out, axis=-1, keepdims=True) + EPS) * fg_ref[...]
    o_ref[...] = out


def _combine(ys, ys_unit, used, info2d, x2d, gate, final_gain, seq, tm, final_norm):
    T, D = x2d.shape
    tiles_per_seq = seq // tm
    cap = _tile_capacity(tm)
    grid_spec = pltpu.PrefetchScalarGridSpec(
        num_scalar_prefetch=2,
        grid=(T // tm,),
        in_specs=[pl.BlockSpec(memory_space=pl.ANY),
                  pl.BlockSpec((tm, LANES), lambda j, yu, us: (j, 0)),
                  pl.BlockSpec((tm, D), lambda j, yu, us: (j, 0)),
                  pl.BlockSpec((None, 1, D), lambda j, yu, us: (j // tiles_per_seq, 0, 0)),
                  pl.BlockSpec((1, D), lambda j, yu, us: (0, 0))],
        out_specs=pl.BlockSpec((tm, D), lambda j, yu, us: (j, 0)),
        scratch_shapes=[pltpu.VMEM((2, cap, D), BF16), pltpu.SemaphoreType.DMA((2,))])
    return pl.pallas_call(
        functools.partial(_combine_kernel, final_norm=final_norm),
        grid_spec=grid_spec,
        out_shape=jax.ShapeDtypeStruct((T, D), F32),
        compiler_params=_params(("arbitrary",)),
        name="moe_combine",
    )(ys_unit, used, ys, info2d, x2d, gate, final_gain)


def _moe_tables(tab, cap, n_blocks):
    nt = tab.shape[0]
    qc = cap // MOE_UNIT
    i32 = jnp.int32
    E = N_EXPERTS
    eids = jnp.arange(E, dtype=i32)

    def pick(onehot, table):
        return jnp.sum(jnp.where(onehot, table, 0), axis=-1)

    units = tab[:, 0, :E]
    first = tab[:, 1, :E]
    cum = jnp.cumsum(units, axis=0)
    excl = cum - units
    ne = cum[-1]
    nb = (ne + UNITS_PER_BLOCK - 1) // UNITS_PER_BLOCK
    bend = jnp.cumsum(nb)
    bstart = bend - nb
    n_used = bend[-1:].astype(i32)
    jb = jnp.arange(n_blocks, dtype=i32)
    block_expert = jnp.minimum(jnp.sum((bend[None, :] <= jb[:, None]).astype(i32), axis=1), E - 1)
    oh_b = block_expert[:, None] == eids[None, :]
    bstart_b = pick(oh_b, bstart[None, :])
    ne_b = pick(oh_b, ne[None, :])
    cum_b = pick(oh_b[:, None, :], cum[None, :, :])
    first_b = pick(oh_b[:, None, :], first[None, :, :])
    excl_b = pick(oh_b[:, None, :], excl[None, :, :])
    u = jnp.arange(UNITS_PER_BLOCK, dtype=i32)
    local = (jb[:, None] - bstart_b[:, None]) * UNITS_PER_BLOCK + u[None, :]
    valid = (jb[:, None] < n_used[0]) & (local < ne_b[:, None])
    t_v = jnp.minimum(jnp.sum((cum_b[:, None, :] <= local[:, :, None]).astype(i32), axis=2), nt - 1)
    oh_t = t_v[:, :, None] == jnp.arange(nt, dtype=i32)[None, None, :]
    src_unit = t_v * qc + pick(oh_t, first_b[:, None, :]) + local - pick(oh_t, excl_b[:, None, :])
    src_unit = jnp.where(valid, src_unit, 0).astype(i32).reshape(-1)

    in_use = jb < n_used[0]
    prev_e = jnp.concatenate([jnp.full((1,), -1, i32), block_expert[:-1]])
    run_first = in_use & (block_expert != prev_e)
    run_slot = (jnp.cumsum(run_first.astype(i32)) - 1) % 2
    run_end = pick(oh_b, bend[None, :])
    next_e = pick(run_end[:, None] == jb[None, :], block_expert[None, :])
    next_e = jnp.where(run_end < n_used[0], next_e, -1)
    wplan = jnp.stack([run_first.astype(i32), run_slot.astype(i32), next_e.astype(i32)], axis=0).reshape(-1)

    q = jnp.arange(qc, dtype=i32)
    gend = first + units
    used = gend[:, -1].astype(i32)
    e_q = jnp.minimum(jnp.sum((gend[:, None, :] <= q[None, :, None]).astype(i32), axis=2), E - 1)
    oh_q = e_q[:, :, None] == eids[None, None, :]
    ys_unit = (UNITS_PER_BLOCK * pick(oh_q, bstart[None, None, :]) + pick(oh_q, excl[:, None, :])
               + q[None, :] - pick(oh_q, first[:, None, :]))
    ys_unit = jnp.where(q[None, :] < used[:, None], ys_unit, 0).astype(i32)
    return block_expert.astype(i32), src_unit, wplan, n_used, ys_unit.reshape(-1), used


def _pick(n, pref):
    t = min(n, pref)
    while n % t:
        t //= 2
    return t


def kernel(x, c, ada_w, ada_b, mix_norm_g, ffn_norm_g, conv_w_pw1, conv_b_pw1, conv_w_dw, conv_b_dw, conv_norm_g, conv_w_pw2, conv_b_pw2, gla_w_in, gla_w_a2, gla_b_a, gla_head_g, gla_w_o, moe_w_group, moe_b_group, moe_w_expert, moe_b_expert, moe_w13, moe_w2, final_norm_g):
    B, S, D = x.shape
    L = ada_w.shape[0]
    T = B * S
    mod = _ada_mod(c, ada_w, ada_b)
    mod = mod.reshape(L, B, N_ADA, 1, D)

    dk = gla_w_a2.shape[2]
    for i in range(L):
        sh1, sc1, g1, sh2, sc2, g2 = [mod[i, :, m] for m in range(N_ADA)]
        j = i // 2
        mg = mix_norm_g[i].reshape(1, D)
        if i % 2 == 0:
            w1 = conv_w_pw1[j].astype(BF16)
            b1 = conv_b_pw1[j].reshape(1, 2 * D)
            u = _norm_glu_proj(x, mg, sc1, sh1, w1[:, :D], w1[:, D:], b1[:, :D], b1[:, D:],
                               tm=_pick(S, 1024), tn=_pick(D, 512))
            x = _conv_tail(u, x, conv_w_dw[j], conv_b_dw[j].reshape(1, D), conv_norm_g[j].reshape(1, D),
                           conv_w_pw2[j].astype(BF16), conv_b_pw2[j].reshape(1, D), g1, tm=_pick(S, 256))
        else:
            w_in = gla_w_in[j]
            n_in = w_in.shape[1]
            tn = PROJ_TILE_N
            n_pad = -(-(n_in - GLA_GATE_RANK + LANES) // tn) * tn
            w_in = jnp.zeros((D, n_pad), F32).at[:, :n_in].set(w_in).astype(BF16)
            proj = _norm_proj(x, mg, sc1, sh1, w_in, tm=_pick(S, 1024), tn=tn)
            wa2 = jnp.zeros((LANES, dk), F32).at[:GLA_GATE_RANK].set(gla_w_a2[j]).astype(BF16)
            gated = _gla_core(proj, wa2, gla_b_a[j].reshape(1, dk),
                              gla_head_g[j].reshape(GLA_HEADS, 1, -1), rows=_pick(S, 512))
            x = _out_proj(gated, x, gla_w_o[j].astype(BF16), g1, tm=_pick(S, 512))

        wr = jnp.zeros((D, LANES), F32).at[:, :N_GROUPS].set(moe_w_group[i])
        wr = wr.at[:, N_GROUPS:N_GROUPS + N_EXPERTS].set(moe_w_expert[i])
        wr_hi = wr.astype(BF16)
        wr_lo = (wr - wr_hi.astype(F32)).astype(BF16)
        br = jnp.zeros((1, LANES), F32).at[0, :N_GROUPS].set(moe_b_group[i])
        br = br.at[0, N_GROUPS:N_GROUPS + N_EXPERTS].set(moe_b_expert[i])
        tm = _pick(S, MOE_TILE)
        cap = _tile_capacity(tm)
        xs, info, tab = _norm_route(x, ffn_norm_g[i].reshape(1, D), sc2, sh2, wr_hi, wr_lo, br, tm=tm)
        n_blocks = -(-(T // tm) * cap // MOE_BLOCK) + N_EXPERTS
        block_expert, src_unit, wplan, n_used, ys_unit, used = _moe_tables(tab, cap, n_blocks)
        ys = _experts(xs.reshape(-1, D + LANES), moe_w13, moe_w2, i, block_expert, src_unit, wplan, n_used)
        x = _combine(ys, ys_unit, used, info.reshape(T, LANES), x.reshape(T, D), g2, final_norm_g.reshape(1, D),
                     S, tm=tm, final_norm=(i == L - 1)).reshape(B, S, D)

    return x
```

```python
import functools

import jax
import jax.numpy as jnp
from jax import lax
from jax.experimental import pallas as pl
from jax.experimental.pallas import tpu as pltpu

EPS = 1e-6
CONV_WIDTH = 31
GLA_HEADS = 4
GLA_GATE_RANK = 16
GLA_TAU = 16.0
GLA_CHUNK = 64
N_GROUPS = 4
EXPERTS_PER_GROUP = 8
N_EXPERTS = N_GROUPS * EXPERTS_PER_GROUP
MOE_BLOCK = 256
MOE_UNIT = 16
UNITS_PER_BLOCK = MOE_BLOCK // MOE_UNIT
MOE_TILE = 512
PROJ_TILE_N = 1280
N_ADA = 6
LANES = 128
SUBLANES = 8
VMEM_LIMIT = 56 * 1024 * 1024

F32 = jnp.float32
BF16 = jnp.bfloat16


def _params(sem):
    return pltpu.CompilerParams(dimension_semantics=sem, vmem_limit_bytes=VMEM_LIMIT)


def _norm_mod(x, gain, scale, shift):
    ms = jnp.mean(x * x, axis=-1, keepdims=True)
    return x * lax.rsqrt(ms + EPS) * gain * (1.0 + scale) + shift


def _silu(x):
    return x * jax.nn.sigmoid(x)


def _ada_kernel(c_ref, w_ref, b_ref, o_ref):
    cond = _silu(c_ref[...])
    o_ref[...] = jnp.dot(cond.astype(BF16), w_ref[...].astype(BF16),
                         preferred_element_type=F32) + b_ref[...]


def _ada_mod(c, ada_w, ada_b):
    L, D, N = ada_w.shape
    B = c.shape[0]
    rows = 8
    c_pad = jnp.zeros((rows, D), F32).at[:B].set(c)
    tn = 1024
    out = pl.pallas_call(
        _ada_kernel,
        grid=(L, N // tn),
        in_specs=[pl.BlockSpec((rows, D), lambda l, n: (0, 0)),
                  pl.BlockSpec((None, D, tn), lambda l, n: (l, 0, n)),
                  pl.BlockSpec((None, 1, tn), lambda l, n: (l, 0, n))],
        out_specs=pl.BlockSpec((None, rows, tn), lambda l, n: (l, 0, n)),
        out_shape=jax.ShapeDtypeStruct((L, rows, N), F32),
        compiler_params=_params(("arbitrary", "arbitrary")),
        name="ada_mod",
    )(c_pad, ada_w, ada_b.reshape(L, 1, N))
    return out[:, :B]


def _proj_kernel(x_ref, g_ref, sc_ref, sh_ref, w_ref, o_ref, h_scr):
    @pl.when(pl.program_id(2) == 0)
    def _():
        h_scr[...] = _norm_mod(x_ref[...], g_ref[...], sc_ref[...], sh_ref[...]).astype(BF16)
    o_ref[...] = jnp.dot(h_scr[...], w_ref[...], preferred_element_type=F32).astype(o_ref.dtype)


def _norm_proj(x, gain, scale, shift, w, tm, tn):
    B, S, D = x.shape
    N = w.shape[1]
    return pl.pallas_call(
        _proj_kernel,
        grid=(B, S // tm, N // tn),
        in_specs=[pl.BlockSpec((None, tm, D), lambda b, s, n: (b, s, 0)),
                  pl.BlockSpec((1, D), lambda b, s, n: (0, 0)),
                  pl.BlockSpec((None, 1, D), lambda b, s, n: (b, 0, 0)),
                  pl.BlockSpec((None, 1, D), lambda b, s, n: (b, 0, 0)),
                  pl.BlockSpec((D, tn), lambda b, s, n: (0, n))],
        out_specs=pl.BlockSpec((None, tm, tn), lambda b, s, n: (b, s, n)),
        out_shape=jax.ShapeDtypeStruct((B, S, N), BF16),
        scratch_shapes=[pltpu.VMEM((tm, D), BF16)],
        compiler_params=_params(("arbitrary", "arbitrary", "arbitrary")),
        name="norm_proj",
    )(x, gain, scale, shift, w)


def _glu_proj_kernel(x_ref, g_ref, sc_ref, sh_ref, wa_ref, wb_ref, ba_ref, bb_ref, o_ref, h_scr):
    @pl.when(pl.program_id(2) == 0)
    def _():
        h_scr[...] = _norm_mod(x_ref[...], g_ref[...], sc_ref[...], sh_ref[...]).astype(BF16)
    h = h_scr[...]
    a = jnp.dot(h, wa_ref[...], preferred_element_type=F32) + ba_ref[...]
    b = jnp.dot(h, wb_ref[...], preferred_element_type=F32) + bb_ref[...]
    o_ref[...] = (a * jax.nn.sigmoid(b)).astype(o_ref.dtype)


def _norm_glu_proj(x, gain, scale, shift, wa, wb, ba, bb, tm, tn):
    B, S, D = x.shape
    N = wa.shape[1]
    vec = pl.BlockSpec((None, 1, D), lambda b, s, n: (b, 0, 0))
    return pl.pallas_call(
        _glu_proj_kernel,
        grid=(B, S // tm, N // tn),
        in_specs=[pl.BlockSpec((None, tm, D), lambda b, s, n: (b, s, 0)),
                  pl.BlockSpec((1, D), lambda b, s, n: (0, 0)), vec, vec,
                  pl.BlockSpec((D, tn), lambda b, s, n: (0, n)),
                  pl.BlockSpec((D, tn), lambda b, s, n: (0, n)),
                  pl.BlockSpec((1, tn), lambda b, s, n: (0, n)),
                  pl.BlockSpec((1, tn), lambda b, s, n: (0, n))],
        out_specs=pl.BlockSpec((None, tm, tn), lambda b, s, n: (b, s, n)),
        out_shape=jax.ShapeDtypeStruct((B, S, N), BF16),
        scratch_shapes=[pltpu.VMEM((tm, D), BF16)],
        compiler_params=_params(("arbitrary", "arbitrary", "arbitrary")),
        name="norm_glu_proj",
    )(x, gain, scale, shift, wa, wb, ba, bb)


def _conv_kernel(u_ref, x_ref, wdw_ref, bdw_ref, ng_ref, w2_ref, b2_ref, g1_ref, o_ref,
                 uf_scr, p_scr, halo_scr, conv_scr):
    tm, D = conv_scr.shape
    G = tm // SUBLANES
    NW = CONV_WIDTH - 1

    @pl.when(pl.program_id(1) == 0)
    def _():
        halo_scr[...] = jnp.zeros_like(halo_scr)

    def grp(j):
        return slice(j * SUBLANES, (j + 1) * SUBLANES)

    n_strips = D // LANES
    for c in range(n_strips):
        uf_scr[c] = u_ref[:, c * LANES:(c + 1) * LANES].astype(F32)
    sub = lax.broadcasted_iota(jnp.int32, (SUBLANES, LANES), 0)

    def strip(c, carry):
        cols = pl.ds(pl.multiple_of(c * LANES, LANES), LANES)
        for j in range(G):
            p_scr[grp(NW + j), cols] = uf_scr[c, pl.ds(j, SUBLANES, stride=G), :]
        for b in range(NW):
            cur = p_scr[grp(b + G), cols]
            prev = halo_scr[grp(b), cols]
            p_scr[grp(b), cols] = pltpu.roll(jnp.where(sub == SUBLANES - 1, prev, cur), 1, axis=0)
        halo_scr[:, cols] = p_scr[G * SUBLANES:(G + NW) * SUBLANES, cols]
        w = [wdw_ref[k:k + 1, cols] for k in range(CONV_WIDTH)]
        bias = bdw_ref[:, cols]
        for j in range(G):
            acc = p_scr[grp(j), cols] * w[0] + bias
            for k in range(1, CONV_WIDTH):
                acc = acc + p_scr[grp(j + k), cols] * w[k]
            conv_scr[grp(j), cols] = acc
        return carry

    lax.fori_loop(0, n_strips, strip, 0)

    v = conv_scr[...]
    ms = jnp.mean(v * v, axis=-1, keepdims=True)
    v = _silu(v * lax.rsqrt(ms + EPS) * ng_ref[...])
    y = jnp.dot(v.astype(BF16), w2_ref[...], preferred_element_type=F32) + b2_ref[...]
    for c in range(n_strips):
        uf_scr[c] = y[:, c * LANES:(c + 1) * LANES]

    def unstride(c, carry):
        cols = pl.ds(pl.multiple_of(c * LANES, LANES), LANES)
        g1 = g1_ref[:, cols]
        for m in range(G):
            t0 = m * SUBLANES
            start = (t0 % G) * SUBLANES + t0 // G
            o_ref[grp(m), cols] = x_ref[grp(m), cols] + g1 * uf_scr[c, pl.ds(start, SUBLANES, stride=SUBLANES), :]
        return carry

    lax.fori_loop(0, n_strips, unstride, 0)


def _conv_tail(u, x, w_dw, b_dw, norm_g, w2, b2, gate, tm):
    B, S, D = x.shape
    G = tm // SUBLANES
    assert G % SUBLANES == 0 and G >= CONV_WIDTH - 1
    row = pl.BlockSpec((1, D), lambda b, s: (0, 0))
    return pl.pallas_call(
        _conv_kernel,
        grid=(B, S // tm),
        in_specs=[pl.BlockSpec((None, tm, D), lambda b, s: (b, s, 0)),
                  pl.BlockSpec((None, tm, D), lambda b, s: (b, s, 0)),
                  pl.BlockSpec((CONV_WIDTH, D), lambda b, s: (0, 0)),
                  row, row,
                  pl.BlockSpec((D, D), lambda b, s: (0, 0)),
                  row,
                  pl.BlockSpec((None, 1, D), lambda b, s: (b, 0, 0))],
        out_specs=pl.BlockSpec((None, tm, D), lambda b, s: (b, s, 0)),
        out_shape=jax.ShapeDtypeStruct((B, S, D), F32),
        scratch_shapes=[pltpu.VMEM((D // LANES, tm, LANES), F32),
                        pltpu.VMEM((tm + (CONV_WIDTH - 1) * SUBLANES, D), F32),
                        pltpu.VMEM(((CONV_WIDTH - 1) * SUBLANES, D), F32),
                        pltpu.VMEM((tm, D), F32)],
        compiler_params=_params(("arbitrary", "arbitrary")),
        name="conv_tail",
    )(u, x, w_dw, b_dw, norm_g, w2, b2, gate)


def _gla_kernel(q_ref, k_ref, v_ref, r_ref, a_ref, wa2_ref, ba_ref, hg_ref, o_ref, st_scr, *, scale):
    @pl.when(pl.program_id(2) == 0)
    def _():
        st_scr[...] = jnp.zeros_like(st_scr)

    rows = q_ref.shape[0]
    z = jnp.dot(a_ref[...], wa2_ref[...], preferred_element_type=F32) + ba_ref[...]
    la = (jnp.minimum(z, 0.0) - jnp.log1p(jnp.exp(-jnp.abs(z)))) * (1.0 / GLA_TAU)

    pos = lax.broadcasted_iota(jnp.int32, la.shape, 0) & (GLA_CHUNK - 1)
    cum = la
    step = 1
    while step < GLA_CHUNK:
        cum = cum + jnp.where(pos >= step, pltpu.roll(cum, step, axis=0), 0.0)
        step *= 2

    ci = lax.broadcasted_iota(jnp.int32, (GLA_CHUNK, GLA_CHUNK), 0)
    cj = lax.broadcasted_iota(jnp.int32, (GLA_CHUNK, GLA_CHUNK), 1)
    causal = cj <= ci
    nt = (((1,), (1,)), ((), ()))
    tn = (((0,), (0,)), ((), ()))
    for c in range(rows // GLA_CHUNK):
        sl = slice(c * GLA_CHUNK, (c + 1) * GLA_CHUNK)
        cum_c = cum[sl]
        last = cum_c[GLA_CHUNK - 1:GLA_CHUNK, :]
        k_f = k_ref[sl, :].astype(F32)
        q_dec = (q_ref[sl, :].astype(F32) * scale * jnp.exp(cum_c)).astype(BF16)
        k_inv = (k_f * jnp.exp(-cum_c)).astype(BF16)
        k_end = (k_f * jnp.exp(last - cum_c)).astype(BF16)
        v_c = v_ref[sl, :]
        att = lax.dot_general(q_dec, k_inv, nt, preferred_element_type=F32)
        att = jnp.where(causal, att, 0.0)
        st = st_scr[...]
        o = (jnp.dot(att.astype(BF16), v_c, preferred_element_type=F32)
             + lax.dot_general(q_dec, st.astype(BF16), nt, preferred_element_type=F32))
        st_scr[...] = jnp.exp(last) * st + lax.dot_general(v_c, k_end, tn, preferred_element_type=F32)
        o = o * lax.rsqrt(jnp.mean(o * o, axis=-1, keepdims=True) + EPS) * hg_ref[...]
        o_ref[sl, :] = (_silu(r_ref[sl, :].astype(F32)) * o).astype(o_ref.dtype)


def _gla_core(proj, wa2, ba, head_g, rows):
    B, S, _ = proj.shape
    H = GLA_HEADS
    dk = wa2.shape[1]
    hk = dk // H
    hv = 2 * hk
    a_blk = (2 * dk + 2 * H * hv) // LANES
    return pl.pallas_call(
        functools.partial(_gla_kernel, scale=float(hk) ** -0.5),
        grid=(B, H, S // rows),
        in_specs=[pl.BlockSpec((None, rows, hk), lambda b, h, t: (b, t, h)),
                  pl.BlockSpec((None, rows, hk), lambda b, h, t: (b, t, H + h)),
                  pl.BlockSpec((None, rows, hv), lambda b, h, t: (b, t, H + h)),
                  pl.BlockSpec((None, rows, hv), lambda b, h, t: (b, t, 2 * H + h)),
                  pl.BlockSpec((None, rows, LANES), lambda b, h, t: (b, t, a_blk)),
                  pl.BlockSpec((LANES, hk), lambda b, h, t: (0, h)),
                  pl.BlockSpec((1, hk), lambda b, h, t: (0, h)),
                  pl.BlockSpec((None, 1, hv), lambda b, h, t: (h, 0, 0))],
        out_specs=pl.BlockSpec((None, rows, hv), lambda b, h, t: (b, t, h)),
        out_shape=jax.ShapeDtypeStruct((B, S, H * hv), BF16),
        scratch_shapes=[pltpu.VMEM((hv, hk), F32)],
        compiler_params=_params(("arbitrary", "arbitrary", "arbitrary")),
        name="gla_core",
    )(proj, proj, proj, proj, proj, wa2, ba, head_g)


def _out_proj_kernel(a_ref, x_ref, w_ref, g_ref, o_ref):
    y = jnp.dot(a_ref[...], w_ref[...], preferred_element_type=F32)
    o_ref[...] = x_ref[...] + g_ref[...] * y


def _out_proj(a, x, w, gate, tm):
    B, S, D = x.shape
    K = a.shape[2]
    return pl.pallas_call(
        _out_proj_kernel,
        grid=(B, S // tm),
        in_specs=[pl.BlockSpec((None, tm, K), lambda b, s: (b, s, 0)),
                  pl.BlockSpec((None, tm, D), lambda b, s: (b, s, 0)),
                  pl.BlockSpec((K, D), lambda b, s: (0, 0)),
                  pl.BlockSpec((None, 1, D), lambda b, s: (b, 0, 0))],
        out_specs=pl.BlockSpec((None, tm, D), lambda b, s: (b, s, 0)),
        out_shape=jax.ShapeDtypeStruct((B, S, D), F32),
        compiler_params=_params(("arbitrary", "arbitrary")),
        name="out_proj",
    )(a, x, w, gate)


def _split3(w):
    hi = w.astype(BF16).astype(F32)
    lo = (w - hi).astype(BF16).astype(F32)
    lo2 = (w - hi - lo).astype(BF16).astype(F32)
    return hi, lo, lo2


def _lane_pick(v, k):
    return v[:, k:k + 1]


def _route_kernel(x_ref, g_ref, sc_ref, sh_ref, wr_ref, br_ref, xs_ref, info_ref, tab_ref,
                  place_scr, h_scr, aux_scr):
    i = pl.program_id(0)

    @pl.when(i == 0)
    def _():
        place_scr[1] = jnp.zeros(place_scr.shape[1:], place_scr.dtype)
        h_scr[1] = jnp.zeros(h_scr.shape[1:], h_scr.dtype)
        aux_scr[1] = jnp.zeros(aux_scr.shape[1:], aux_scr.dtype)

    def body(cur, prv):
        placing = _place_stage(place_scr[prv], h_scr[prv], aux_scr[prv], xs_ref)
        routing = _route_stage(x_ref, g_ref, sc_ref, sh_ref, wr_ref, br_ref, info_ref, tab_ref,
                               place_scr.shape[1])
        routed = None
        while placing is not None or routing is not None:
            if placing is not None:
                try:
                    next(placing)
                except StopIteration:
                    placing = None
            if routing is not None:
                try:
                    next(routing)
                except StopIteration as done:
                    routed, routing = done.value, None
        place_scr[cur], h_scr[cur], aux_scr[cur] = routed

    @pl.when(i % 2 == 0)
    def _():
        body(0, 1)

    @pl.when(i % 2 == 1)
    def _():
        body(1, 0)


def _route_stage(x_ref, g_ref, sc_ref, sh_ref, wr_ref, br_ref, info_ref, tab_ref, cap):
    tm, D = x_ref.shape
    h = _norm_mod(x_ref[...], g_ref[...], sc_ref[...], sh_ref[...])
    h_hi = h.astype(BF16)
    h_lo = (h - h_hi.astype(F32)).astype(BF16)
    both = jnp.dot(h_hi, wr_ref[...], preferred_element_type=F32)
    logits = (both[:, :LANES] + both[:, LANES:]
              + jnp.dot(h_lo, wr_ref[:, :LANES], preferred_element_type=F32)) + br_ref[...]
    yield
    lane = lax.broadcasted_iota(jnp.int32, logits.shape, 1)
    neg = -jnp.inf
    gmask = lane < N_GROUPS
    gl = jnp.where(gmask, logits, neg)
    gm = jnp.max(gl, axis=-1, keepdims=True)
    gsum = jnp.sum(jnp.where(gmask, jnp.exp(logits - gm), 0.0), axis=-1, keepdims=True)
    g_val = 1.0 / gsum
    g_idx = jnp.min(jnp.where(gl == gm, lane, LANES), axis=-1, keepdims=True)
    lo = N_GROUPS + EXPERTS_PER_GROUP * g_idx
    emask = (lane >= lo) & (lane < lo + EXPERTS_PER_GROUP)
    el = jnp.where(emask, logits, neg)
    m1 = jnp.max(el, axis=-1, keepdims=True)
    i1 = jnp.min(jnp.where(el == m1, lane, LANES), axis=-1, keepdims=True)
    el2 = jnp.where(lane == i1, neg, el)
    m2 = jnp.max(el2, axis=-1, keepdims=True)
    i2 = jnp.min(jnp.where(el2 == m2, lane, LANES), axis=-1, keepdims=True)
    e2 = jnp.exp(m2 - m1)
    w1 = g_val / (1.0 + e2)
    w2 = g_val * e2 / (1.0 + e2)
    yield
    oh1 = lane == (i1 - N_GROUPS)
    oh2 = lane == (i2 - N_GROUPS)
    oh = jnp.where(oh1, 1.0, jnp.where(oh2, 1.0, 0.0))
    ri = lax.broadcasted_iota(jnp.int32, (tm, tm), 0)
    ci = lax.broadcasted_iota(jnp.int32, (tm, tm), 1)
    before = jnp.where(ri > ci, 1.0, 0.0).astype(BF16)
    rank = jnp.dot(before, oh.astype(BF16), preferred_element_type=F32)
    count = jnp.sum(oh, axis=0, keepdims=True)
    units = jnp.floor((count + (MOE_UNIT - 1.0)) * (1.0 / MOE_UNIT))
    ui = lax.broadcasted_iota(jnp.int32, (LANES, LANES), 0)
    uj = lax.broadcasted_iota(jnp.int32, (LANES, LANES), 1)
    earlier = jnp.where(ui < uj, 1.0, 0.0).astype(BF16)
    first_unit = jnp.dot(jnp.broadcast_to(units, (8, LANES)).astype(BF16), earlier,
                         preferred_element_type=F32)[0:1]
    pos = rank + first_unit * MOE_UNIT
    slot1 = jnp.sum(jnp.where(oh1, pos, 0.0), axis=-1, keepdims=True)
    slot2 = jnp.sum(jnp.where(oh2, pos, 0.0), axis=-1, keepdims=True)
    info = jnp.where(lane == 0, w1, 0.0)
    info = jnp.where(lane == 1, w2, info)
    info = jnp.where(lane == 2, slot1, info)
    info = jnp.where(lane == 3, slot2, info)
    info_ref[...] = info
    row8 = lax.broadcasted_iota(jnp.int32, (8, LANES), 0)
    tab = jnp.where(row8 == 0, units, jnp.where(row8 == 1, first_unit, 0.0))
    tab_ref[...] = tab.astype(jnp.int32)
    yield

    info_t = info.T
    s1_row = info_t[2:3, :]
    s2_row = info_t[3:4, :]
    r_idx = lax.broadcasted_iota(jnp.int32, (cap, tm), 0).astype(F32)
    place = jnp.where(r_idx == s1_row, 1.0, jnp.where(r_idx == s2_row, 1.0, 0.0)).astype(BF16)
    w1s = _split3(w1)
    w2s = _split3(w2)
    s1_hi = jnp.floor(slot1 * (1.0 / MOE_UNIT))
    s1_lo = slot1 - s1_hi * MOE_UNIT
    aux = jnp.zeros((tm, LANES), F32)
    for k, col in enumerate(w1s + w2s + (s1_hi, s1_lo)):
        aux = jnp.where(lane == k, col, aux)
    return place, h.astype(BF16), aux.astype(BF16)


def _place_stage(place, h_bf, aux, xs_ref):
    cap, tm = place.shape
    D = h_bf.shape[1]
    cw = 512 if D % 512 == 0 else LANES
    for c0 in range(0, D, cw):
        xs_ref[:, c0:c0 + cw] = jnp.dot(place, h_bf[:, c0:c0 + cw], preferred_element_type=F32).astype(BF16)
        yield
    got = jnp.dot(place, aux, preferred_element_type=F32)
    row_slot1 = _lane_pick(got, 6) * MOE_UNIT + _lane_pick(got, 7)
    r_col = lax.broadcasted_iota(jnp.int32, (cap, 1), 0).astype(F32)
    w_row = jnp.where(row_slot1 == r_col,
                      _lane_pick(got, 0) + _lane_pick(got, 1) + _lane_pick(got, 2),
                      _lane_pick(got, 3) + _lane_pick(got, 4) + _lane_pick(got, 5))
    lane_c = lax.broadcasted_iota(jnp.int32, (cap, LANES), 1)
    wcols = jnp.zeros((cap, LANES), F32)
    for k, col in enumerate(_split3(w_row)):
        wcols = jnp.where(lane_c == k, col, wcols)
    xs_ref[:, D:D + LANES] = wcols.astype(BF16)


def _tile_capacity(tm):
    cap = 2 * tm + N_EXPERTS * (MOE_UNIT - 1)
    return -(-cap // LANES) * LANES


def _norm_route(x, gain, scale, shift, wr, br, tm):
    B, S, D = x.shape
    spt = S // tm
    nt = B * spt
    cap = _tile_capacity(tm)
    routed = lambda i: jnp.minimum(i, nt - 1)
    placed = lambda i: jnp.maximum(i - 1, 0)
    vec = pl.BlockSpec((None, 1, D), lambda i: (routed(i) // spt, 0, 0))
    return pl.pallas_call(
        _route_kernel,
        grid=(nt + 1,),
        in_specs=[pl.BlockSpec((tm, D), lambda i: (routed(i), 0)),
                  pl.BlockSpec((1, D), lambda i: (0, 0)), vec, vec,
                  pl.BlockSpec((D, 2 * LANES), lambda i: (0, 0)),
                  pl.BlockSpec((1, LANES), lambda i: (0, 0))],
        out_specs=[pl.BlockSpec((None, cap, D + LANES), lambda i: (placed(i), 0, 0)),
                   pl.BlockSpec((tm, LANES), lambda i: (routed(i), 0)),
                   pl.BlockSpec((None, 8, LANES), lambda i: (routed(i), 0, 0))],
        out_shape=[jax.ShapeDtypeStruct((nt, cap, D + LANES), BF16),
                   jax.ShapeDtypeStruct((B * S, LANES), F32),
                   jax.ShapeDtypeStruct((nt, 8, LANES), jnp.int32)],
        scratch_shapes=[pltpu.VMEM((2, cap, tm), BF16),
                        pltpu.VMEM((2, tm, D), BF16),
                        pltpu.VMEM((2, tm, LANES), BF16)],
        compiler_params=_params(("arbitrary",)),
        name="norm_route",
    )(x.reshape(B * S, D), gain, scale, shift, wr, br)


def _expert_kernel(be_ref, src_ref, wplan_ref, nused_ref, xs_hbm, w13_hbm, w2_hbm, ys_ref,
                   xbuf, sem, w13f, w2f, wsem, w13b, w2b, *, layer):
    i = pl.program_id(0)
    nb = pl.num_programs(0)
    n_used = nused_ref[0]
    de = w2b.shape[0]
    D = w13b.shape[0]

    def weight_copies(expert, slot):
        return (pltpu.make_async_copy(w13_hbm.at[layer, expert], w13f.at[slot], wsem.at[0, slot]),
                pltpu.make_async_copy(w2_hbm.at[layer, expert], w2f.at[slot], wsem.at[1, slot]))

    def unit_copy(block, u, slot):
        src = pl.multiple_of(src_ref[block * UNITS_PER_BLOCK + u] * MOE_UNIT, MOE_UNIT)
        return pltpu.make_async_copy(xs_hbm.at[pl.ds(src, MOE_UNIT)],
                                     xbuf.at[slot, pl.ds(u * MOE_UNIT, MOE_UNIT)], sem.at[slot])

    def issue(block, slot):
        for u in range(UNITS_PER_BLOCK):
            unit_copy(block, u, slot).start()

    @pl.when((i == 0) & (n_used > 0))
    def _():
        issue(0, 0)

    @pl.when(i + 1 < n_used)
    def _():
        issue(i + 1, (i + 1) % 2)

    @pl.when(i < n_used)
    def _():
        slot = i % 2
        for u in range(UNITS_PER_BLOCK):
            unit_copy(i, u, slot).wait()

        @pl.when(wplan_ref[i] == 1)
        def _():
            expert = be_ref[i]
            wslot = wplan_ref[nb + i]
            nxt = wplan_ref[2 * nb + i]

            @pl.when(i == 0)
            def _():
                for cp in weight_copies(expert, wslot):
                    cp.start()

            for cp in weight_copies(expert, wslot):
                cp.wait()
            w13b[...] = w13f[wslot].astype(BF16)
            w2b[...] = w2f[wslot].astype(BF16)

            @pl.when(nxt >= 0)
            def _():
                for cp in weight_copies(nxt, 1 - wslot):
                    cp.start()

        xb = xbuf[slot, :, 0:D]
        wc = xbuf[slot, :, D:D + LANES].astype(F32)
        w_row = _lane_pick(wc, 0) + _lane_pick(wc, 1) + _lane_pick(wc, 2)
        ag = jnp.dot(xb, w13b[...], preferred_element_type=F32)
        act = (_silu(ag[:, :de]) * ag[:, de:]).astype(BF16)
        ys_ref[...] = (w_row * jnp.dot(act, w2b[...], preferred_element_type=F32)).astype(ys_ref.dtype)

    @pl.when(i >= n_used)
    def _():
        ys_ref[...] = jnp.zeros_like(ys_ref)


def _experts(xs2d, w13, w2, layer, block_expert, src_unit, wplan, n_used):
    D = xs2d.shape[1] - LANES
    n_blocks = block_expert.shape[0]
    de = w2.shape[2]
    grid_spec = pltpu.PrefetchScalarGridSpec(
        num_scalar_prefetch=4,
        grid=(n_blocks,),
        in_specs=[pl.BlockSpec(memory_space=pl.ANY),
                  pl.BlockSpec(memory_space=pl.ANY),
                  pl.BlockSpec(memory_space=pl.ANY)],
        out_specs=pl.BlockSpec((MOE_BLOCK, D), lambda i, be, src, wp, nu: (i, 0)),
        scratch_shapes=[pltpu.VMEM((2, MOE_BLOCK, D + LANES), BF16),
                        pltpu.SemaphoreType.DMA((2,)),
                        pltpu.VMEM((2, D, 2 * de), F32),
                        pltpu.VMEM((2, de, D), F32),
                        pltpu.SemaphoreType.DMA((2, 2)),
                        pltpu.VMEM((D, 2 * de), BF16),
                        pltpu.VMEM((de, D), BF16)])
    return pl.pallas_call(
        functools.partial(_expert_kernel, layer=layer),
        grid_spec=grid_spec,
        out_shape=jax.ShapeDtypeStruct((n_blocks * MOE_BLOCK, D), BF16),
        compiler_params=_params(("arbitrary",)),
        name="moe_experts",
    )(block_expert, src_unit, wplan, n_used, xs2d, w13, w2)


def _combine_kernel(yu_ref, used_ref, ys_hbm, info_ref, x_ref, g_ref, fg_ref, o_ref, ybuf, sem, *, final_norm):
    j = pl.program_id(0)
    nj = pl.num_programs(0)
    tm = x_ref.shape[0]
    cap = ybuf.shape[1]
    n_units = cap // MOE_UNIT

    def unit_copy(tile, q, slot):
        src = pl.multiple_of(yu_ref[tile * n_units + q] * MOE_UNIT, MOE_UNIT)
        dst = pl.multiple_of(q * MOE_UNIT, MOE_UNIT)
        return pltpu.make_async_copy(ys_hbm.at[pl.ds(src, MOE_UNIT)], ybuf.at[slot, pl.ds(dst, MOE_UNIT)], sem.at[slot])

    def issue(tile, slot):
        def body(q, carry):
            unit_copy(tile, q, slot).start()
            return carry
        lax.fori_loop(0, used_ref[tile], body, 0)

    @pl.when(j == 0)
    def _():
        issue(0, 0)

    @pl.when(j + 1 < nj)
    def _():
        issue(j + 1, (j + 1) % 2)

    slot = j % 2
    used = used_ref[j]

    def wait_body(q, carry):
        unit_copy(j, q, slot).wait()
        return carry
    lax.fori_loop(0, used, wait_body, 0)

    def clear_body(q, carry):
        ybuf[slot, pl.ds(pl.multiple_of(q * MOE_UNIT, MOE_UNIT), MOE_UNIT), :] = jnp.zeros((MOE_UNIT, ybuf.shape[2]), ybuf.dtype)
        return carry
    lax.fori_loop(used, n_units, clear_body, 0)

    info = info_ref[...]
    slot1 = _lane_pick(info, 2)
    slot2 = _lane_pick(info, 3)
    r_idx = lax.broadcasted_iota(jnp.int32, (tm, cap), 1).astype(F32)
    pick = jnp.where(r_idx == slot1, 1.0, jnp.where(r_idx == slot2, 1.0, 0.0)).astype(BF16)
    moe = jnp.dot(pick, ybuf[slot], preferred_element_type=F32)
    out = x_ref[...] + g_ref[...] * moe
    if final_norm:
        out = out * lax.rsqrt(jnp.mean(out * out, axis=-1, keepdims=True) + EPS) * fg_ref[...]
    o_ref[...] = out


def _combine(ys, ys_unit, used, info2d, x2d, gate, final_gain, seq, tm, final_norm):
    T, D = x2d.shape
    tiles_per_seq = seq // tm
    cap = _tile_capacity(tm)
    grid_spec = pltpu.PrefetchScalarGridSpec(
        num_scalar_prefetch=2,
        grid=(T // tm,),
        in_specs=[pl.BlockSpec(memory_space=pl.ANY),
                  pl.BlockSpec((tm, LANES), lambda j, yu, us: (j, 0)),
                  pl.BlockSpec((tm, D), lambda j, yu, us: (j, 0)),
                  pl.BlockSpec((None, 1, D), lambda j, yu, us: (j // tiles_per_seq, 0, 0)),
                  pl.BlockSpec((1, D), lambda j, yu, us: (0, 0))],
        out_specs=pl.BlockSpec((tm, D), lambda j, yu, us: (j, 0)),
        scratch_shapes=[pltpu.VMEM((2, cap, D), BF16), pltpu.SemaphoreType.DMA((2,))])
    return pl.pallas_call(
        functools.partial(_combine_kernel, final_norm=final_norm),
        grid_spec=grid_spec,
        out_shape=jax.ShapeDtypeStruct((T, D), F32),
        compiler_params=_params(("arbitrary",)),
        name="moe_combine",
    )(ys_unit, used, ys, info2d, x2d, gate, final_gain)


def _moe_tables(tab, cap, n_blocks):
    nt = tab.shape[0]
    qc = cap // MOE_UNIT
    i32 = jnp.int32
    E = N_EXPERTS
    eids = jnp.arange(E, dtype=i32)

    def pick(onehot, table):
        return jnp.sum(jnp.where(onehot, table, 0), axis=-1)

    units = tab[:, 0, :E]
    first = tab[:, 1, :E]
    cum = jnp.cumsum(units, axis=0)
    excl = cum - units
    ne = cum[-1]
    nb = (ne + UNITS_PER_BLOCK - 1) // UNITS_PER_BLOCK
    bend = jnp.cumsum(nb)
    bstart = bend - nb
    n_used = bend[-1:].astype(i32)
    jb = jnp.arange(n_blocks, dtype=i32)
    block_expert = jnp.minimum(jnp.sum((bend[None, :] <= jb[:, None]).astype(i32), axis=1), E - 1)
    oh_b = block_expert[:, None] == eids[None, :]
    bstart_b = pick(oh_b, bstart[None, :])
    ne_b = pick(oh_b, ne[None, :])
    cum_b = pick(oh_b[:, None, :], cum[None, :, :])
    first_b = pick(oh_b[:, None, :], first[None, :, :])
    excl_b = pick(oh_b[:, None, :], excl[None, :, :])
    u = jnp.arange(UNITS_PER_BLOCK, dtype=i32)
    local = (jb[:, None] - bstart_b[:, None]) * UNITS_PER_BLOCK + u[None, :]
    valid = (jb[:, None] < n_used[0]) & (local < ne_b[:, None])
    t_v = jnp.minimum(jnp.sum((cum_b[:, None, :] <= local[:, :, None]).astype(i32), axis=2), nt - 1)
    oh_t = t_v[:, :, None] == jnp.arange(nt, dtype=i32)[None, None, :]
    src_unit = t_v * qc + pick(oh_t, first_b[:, None, :]) + local - pick(oh_t, excl_b[:, None, :])
    src_unit = jnp.where(valid, src_unit, 0).astype(i32).reshape(-1)

    in_use = jb < n_used[0]
    prev_e = jnp.concatenate([jnp.full((1,), -1, i32), block_expert[:-1]])
    run_first = in_use & (block_expert != prev_e)
    run_slot = (jnp.cumsum(run_first.astype(i32)) - 1) % 2
    run_end = pick(oh_b, bend[None, :])
    next_e = pick(run_end[:, None] == jb[None, :], block_expert[None, :])
    next_e = jnp.where(run_end < n_used[0], next_e, -1)
    wplan = jnp.stack([run_first.astype(i32), run_slot.astype(i32), next_e.astype(i32)], axis=0).reshape(-1)

    q = jnp.arange(qc, dtype=i32)
    gend = first + units
    used = gend[:, -1].astype(i32)
    e_q = jnp.minimum(jnp.sum((gend[:, None, :] <= q[None, :, None]).astype(i32), axis=2), E - 1)
    oh_q = e_q[:, :, None] == eids[None, None, :]
    ys_unit = (UNITS_PER_BLOCK * pick(oh_q, bstart[None, None, :]) + pick(oh_q, excl[:, None, :])
               + q[None, :] - pick(oh_q, first[:, None, :]))
    ys_unit = jnp.where(q[None, :] < used[:, None], ys_unit, 0).astype(i32)
    return block_expert.astype(i32), src_unit, wplan, n_used, ys_unit.reshape(-1), used


def _pick(n, pref):
    t = min(n, pref)
    while n % t:
        t //= 2
    return t


def kernel(x, c, ada_w, ada_b, mix_norm_g, ffn_norm_g, conv_w_pw1, conv_b_pw1, conv_w_dw, conv_b_dw, conv_norm_g, conv_w_pw2, conv_b_pw2, gla_w_in, gla_w_a2, gla_b_a, gla_head_g, gla_w_o, moe_w_group, moe_b_group, moe_w_expert, moe_b_expert, moe_w13, moe_w2, final_norm_g):
    B, S, D = x.shape
    L = ada_w.shape[0]
    T = B * S
    mod = _ada_mod(c, ada_w, ada_b)
    mod = mod.reshape(L, B, N_ADA, 1, D)

    dk = gla_w_a2.shape[2]
    for i in range(L):
        sh1, sc1, g1, sh2, sc2, g2 = [mod[i, :, m] for m in range(N_ADA)]
        j = i // 2
        mg = mix_norm_g[i].reshape(1, D)
        if i % 2 == 0:
            w1 = conv_w_pw1[j].astype(BF16)
            b1 = conv_b_pw1[j].reshape(1, 2 * D)
            u = _norm_glu_proj(x, mg, sc1, sh1, w1[:, :D], w1[:, D:], b1[:, :D], b1[:, D:],
                               tm=_pick(S, 1024), tn=_pick(D, 512))
            x = _conv_tail(u, x, conv_w_dw[j], conv_b_dw[j].reshape(1, D), conv_norm_g[j].reshape(1, D),
                           conv_w_pw2[j].astype(BF16), conv_b_pw2[j].reshape(1, D), g1, tm=_pick(S, 256))
        else:
            w_in = gla_w_in[j]
            n_in = w_in.shape[1]
            tn = PROJ_TILE_N
            n_pad = -(-(n_in - GLA_GATE_RANK + LANES) // tn) * tn
            w_in = jnp.zeros((D, n_pad), F32).at[:, :n_in].set(w_in).astype(BF16)
            proj = _norm_proj(x, mg, sc1, sh1, w_in, tm=_pick(S, 1024), tn=tn)
            wa2 = jnp.zeros((LANES, dk), F32).at[:GLA_GATE_RANK].set(gla_w_a2[j]).astype(BF16)
            gated = _gla_core(proj, wa2, gla_b_a[j].reshape(1, dk),
                              gla_head_g[j].reshape(GLA_HEADS, 1, -1), rows=_pick(S, 512))
            x = _out_proj(gated, x, gla_w_o[j].astype(BF16), g1, tm=_pick(S, 512))

        wr = jnp.zeros((D, LANES), F32).at[:, :N_GROUPS].set(moe_w_group[i])
        wr = wr.at[:, N_GROUPS:N_GROUPS + N_EXPERTS].set(moe_w_expert[i])
        wr_hi = wr.astype(BF16)
        wr_lo = (wr - wr_hi.astype(F32)).astype(BF16)
        wr_cat = jnp.concatenate([wr_hi, wr_lo], axis=1)
        br = jnp.zeros((1, LANES), F32).at[0, :N_GROUPS].set(moe_b_group[i])
        br = br.at[0, N_GROUPS:N_GROUPS + N_EXPERTS].set(moe_b_expert[i])
        tm = _pick(S, MOE_TILE)
        cap = _tile_capacity(tm)
        xs, info, tab = _norm_route(x, ffn_norm_g[i].reshape(1, D), sc2, sh2, wr_cat, br, tm=tm)
        n_blocks = -(-(T // tm) * cap // MOE_BLOCK) + N_EXPERTS
        block_expert, src_unit, wplan, n_used, ys_unit, used = _moe_tables(tab, cap, n_blocks)
        ys = _experts(xs.reshape(-1, D + LANES), moe_w13, moe_w2, i, block_expert, src_unit, wplan, n_used)
        x = _combine(ys, ys_unit, used, info.reshape(T, LANES), x.reshape(T, D), g2, final_norm_g.reshape(1, D),
                     S, tm=tm, final_norm=(i == L - 1)).reshape(B, S, D)

    return x
```

```python
import functools

import jax
import jax.numpy as jnp
from jax import lax
from jax.experimental import pallas as pl
from jax.experimental.pallas import tpu as pltpu

EPS = 1e-6
CONV_WIDTH = 31
GLA_HEADS = 4
GLA_HEADS_PER_STEP = 2
GLA_GATE_RANK = 16
GLA_TAU = 16.0
GLA_CHUNK = 64
N_GROUPS = 4
EXPERTS_PER_GROUP = 8
N_EXPERTS = N_GROUPS * EXPERTS_PER_GROUP
MOE_BLOCK = 256
MOE_UNIT = 16
UNITS_PER_BLOCK = MOE_BLOCK // MOE_UNIT
MOE_TILE = 512
PROJ_TILE_N = 1280
N_ADA = 6
LANES = 128
SUBLANES = 8
VMEM_LIMIT = 56 * 1024 * 1024

F32 = jnp.float32
BF16 = jnp.bfloat16


def _params(sem):
    return pltpu.CompilerParams(dimension_semantics=sem, vmem_limit_bytes=VMEM_LIMIT)


def _norm_mod(x, gain, scale, shift):
    ms = jnp.mean(x * x, axis=-1, keepdims=True)
    return x * lax.rsqrt(ms + EPS) * gain * (1.0 + scale) + shift


def _silu(x):
    return x * jax.nn.sigmoid(x)


def _ada_kernel(c_ref, w_ref, b_ref, o_ref):
    cond = _silu(c_ref[...])
    o_ref[...] = jnp.dot(cond.astype(BF16), w_ref[...].astype(BF16),
                         preferred_element_type=F32) + b_ref[...]


def _ada_mod(c, ada_w, ada_b):
    L, D, N = ada_w.shape
    B = c.shape[0]
    rows = 8
    c_pad = jnp.zeros((rows, D), F32).at[:B].set(c)
    tn = 1024
    out = pl.pallas_call(
        _ada_kernel,
        grid=(L, N // tn),
        in_specs=[pl.BlockSpec((rows, D), lambda l, n: (0, 0)),
                  pl.BlockSpec((None, D, tn), lambda l, n: (l, 0, n)),
                  pl.BlockSpec((None, 1, tn), lambda l, n: (l, 0, n))],
        out_specs=pl.BlockSpec((None, rows, tn), lambda l, n: (l, 0, n)),
        out_shape=jax.ShapeDtypeStruct((L, rows, N), F32),
        compiler_params=_params(("arbitrary", "arbitrary")),
        name="ada_mod",
    )(c_pad, ada_w, ada_b.reshape(L, 1, N))
    return out[:, :B]


def _proj_kernel(x_ref, g_ref, sc_ref, sh_ref, w_ref, o_ref, h_scr):
    @pl.when(pl.program_id(2) == 0)
    def _():
        h_scr[...] = _norm_mod(x_ref[...], g_ref[...], sc_ref[...], sh_ref[...]).astype(BF16)
    o_ref[...] = jnp.dot(h_scr[...], w_ref[...], preferred_element_type=F32).astype(o_ref.dtype)


def _norm_proj(x, gain, scale, shift, w, tm, tn):
    B, S, D = x.shape
    N = w.shape[1]
    return pl.pallas_call(
        _proj_kernel,
        grid=(B, S // tm, N // tn),
        in_specs=[pl.BlockSpec((None, tm, D), lambda b, s, n: (b, s, 0)),
                  pl.BlockSpec((1, D), lambda b, s, n: (0, 0)),
                  pl.BlockSpec((None, 1, D), lambda b, s, n: (b, 0, 0)),
                  pl.BlockSpec((None, 1, D), lambda b, s, n: (b, 0, 0)),
                  pl.BlockSpec((D, tn), lambda b, s, n: (0, n))],
        out_specs=pl.BlockSpec((None, tm, tn), lambda b, s, n: (b, s, n)),
        out_shape=jax.ShapeDtypeStruct((B, S, N), BF16),
        scratch_shapes=[pltpu.VMEM((tm, D), BF16)],
        compiler_params=_params(("arbitrary", "arbitrary", "arbitrary")),
        name="norm_proj",
    )(x, gain, scale, shift, w)


def _glu_proj_kernel(x_ref, g_ref, sc_ref, sh_ref, wa_ref, wb_ref, ba_ref, bb_ref, o_ref, h_scr):
    @pl.when(pl.program_id(2) == 0)
    def _():
        h_scr[...] = _norm_mod(x_ref[...], g_ref[...], sc_ref[...], sh_ref[...]).astype(BF16)
    h = h_scr[...]
    a = jnp.dot(h, wa_ref[...], preferred_element_type=F32) + ba_ref[...]
    b = jnp.dot(h, wb_ref[...], preferred_element_type=F32) + bb_ref[...]
    o_ref[...] = (a * jax.nn.sigmoid(b)).astype(o_ref.dtype)


def _norm_glu_proj(x, gain, scale, shift, wa, wb, ba, bb, tm, tn):
    B, S, D = x.shape
    N = wa.shape[1]
    vec = pl.BlockSpec((None, 1, D), lambda b, s, n: (b, 0, 0))
    return pl.pallas_call(
        _glu_proj_kernel,
        grid=(B, S // tm, N // tn),
        in_specs=[pl.BlockSpec((None, tm, D), lambda b, s, n: (b, s, 0)),
                  pl.BlockSpec((1, D), lambda b, s, n: (0, 0)), vec, vec,
                  pl.BlockSpec((D, tn), lambda b, s, n: (0, n)),
                  pl.BlockSpec((D, tn), lambda b, s, n: (0, n)),
                  pl.BlockSpec((1, tn), lambda b, s, n: (0, n)),
                  pl.BlockSpec((1, tn), lambda b, s, n: (0, n))],
        out_specs=pl.BlockSpec((None, tm, tn), lambda b, s, n: (b, s, n)),
        out_shape=jax.ShapeDtypeStruct((B, S, N), BF16),
        scratch_shapes=[pltpu.VMEM((tm, D), BF16)],
        compiler_params=_params(("arbitrary", "arbitrary", "arbitrary")),
        name="norm_glu_proj",
    )(x, gain, scale, shift, wa, wb, ba, bb)


def _conv_kernel(u_ref, x_ref, wdw_ref, bdw_ref, ng_ref, w2_ref, b2_ref, g1_ref, o_ref,
                 uf_scr, p_scr, halo_scr, conv_scr):
    tm, D = conv_scr.shape
    G = tm // SUBLANES
    NW = CONV_WIDTH - 1

    @pl.when(pl.program_id(1) == 0)
    def _():
        halo_scr[...] = jnp.zeros_like(halo_scr)

    def grp(j):
        return slice(j * SUBLANES, (j + 1) * SUBLANES)

    n_strips = D // LANES
    for c in range(n_strips):
        uf_scr[c] = u_ref[:, c * LANES:(c + 1) * LANES].astype(F32)
    sub = lax.broadcasted_iota(jnp.int32, (SUBLANES, LANES), 0)

    def strip(c, carry):
        cols = pl.ds(pl.multiple_of(c * LANES, LANES), LANES)
        for j in range(G):
            p_scr[grp(NW + j), cols] = uf_scr[c, pl.ds(j, SUBLANES, stride=G), :]
        for b in range(NW):
            cur = p_scr[grp(b + G), cols]
            prev = halo_scr[grp(b), cols]
            p_scr[grp(b), cols] = pltpu.roll(jnp.where(sub == SUBLANES - 1, prev, cur), 1, axis=0)
        halo_scr[:, cols] = p_scr[G * SUBLANES:(G + NW) * SUBLANES, cols]
        w = [wdw_ref[k:k + 1, cols] for k in range(CONV_WIDTH)]
        bias = bdw_ref[:, cols]
        for j in range(G):
            acc = p_scr[grp(j), cols] * w[0] + bias
            for k in range(1, CONV_WIDTH):
                acc = acc + p_scr[grp(j + k), cols] * w[k]
            conv_scr[grp(j), cols] = acc
        return carry

    lax.fori_loop(0, n_strips, strip, 0)

    v = conv_scr[...]
    ms = jnp.mean(v * v, axis=-1, keepdims=True)
    v = _silu(v * lax.rsqrt(ms + EPS) * ng_ref[...])
    y = jnp.dot(v.astype(BF16), w2_ref[...], preferred_element_type=F32) + b2_ref[...]
    for c in range(n_strips):
        uf_scr[c] = y[:, c * LANES:(c + 1) * LANES]

    def unstride(c, carry):
        cols = pl.ds(pl.multiple_of(c * LANES, LANES), LANES)
        g1 = g1_ref[:, cols]
        for m in range(G):
            t0 = m * SUBLANES
            start = (t0 % G) * SUBLANES + t0 // G
            o_ref[grp(m), cols] = x_ref[grp(m), cols] + g1 * uf_scr[c, pl.ds(start, SUBLANES, stride=SUBLANES), :]
        return carry

    lax.fori_loop(0, n_strips, unstride, 0)


def _conv_tail(u, x, w_dw, b_dw, norm_g, w2, b2, gate, tm):
    B, S, D = x.shape
    G = tm // SUBLANES
    assert G % SUBLANES == 0 and G >= CONV_WIDTH - 1
    row = pl.BlockSpec((1, D), lambda b, s: (0, 0))
    return pl.pallas_call(
        _conv_kernel,
        grid=(B, S // tm),
        in_specs=[pl.BlockSpec((None, tm, D), lambda b, s: (b, s, 0)),
                  pl.BlockSpec((None, tm, D), lambda b, s: (b, s, 0)),
                  pl.BlockSpec((CONV_WIDTH, D), lambda b, s: (0, 0)),
                  row, row,
                  pl.BlockSpec((D, D), lambda b, s: (0, 0)),
                  row,
                  pl.BlockSpec((None, 1, D), lambda b, s: (b, 0, 0))],
        out_specs=pl.BlockSpec((None, tm, D), lambda b, s: (b, s, 0)),
        out_shape=jax.ShapeDtypeStruct((B, S, D), F32),
        scratch_shapes=[pltpu.VMEM((D // LANES, tm, LANES), F32),
                        pltpu.VMEM((tm + (CONV_WIDTH - 1) * SUBLANES, D), F32),
                        pltpu.VMEM(((CONV_WIDTH - 1) * SUBLANES, D), F32),
                        pltpu.VMEM((tm, D), F32)],
        compiler_params=_params(("arbitrary", "arbitrary")),
        name="conv_tail",
    )(u, x, w_dw, b_dw, norm_g, w2, b2, gate)


def _gla_head(e, cum, q_ref, k_ref, v_ref, r_ref, hg_ref, o_ref, st_scr, scale):
    rows = q_ref.shape[0]
    hv, hk = st_scr.shape[1:]
    kc = slice(e * hk, (e + 1) * hk)
    vc = slice(e * hv, (e + 1) * hv)
    ci = lax.broadcasted_iota(jnp.int32, (GLA_CHUNK, GLA_CHUNK), 0)
    cj = lax.broadcasted_iota(jnp.int32, (GLA_CHUNK, GLA_CHUNK), 1)
    causal = cj <= ci
    nt = (((1,), (1,)), ((), ()))
    tn = (((0,), (0,)), ((), ()))
    C = GLA_CHUNK

    def dot_nt(a, b):
        return lax.dot_general(a, b, nt, preferred_element_type=F32)

    for p in range(rows // (2 * C)):
        sl_a = slice(2 * p * C, (2 * p + 1) * C)
        sl_b = slice((2 * p + 1) * C, (2 * p + 2) * C)
        sl_ab = slice(2 * p * C, (2 * p + 2) * C)
        cum_a, cum_b = cum[sl_a, kc], cum[sl_b, kc]
        last_a, last_b = cum_a[C - 1:C, :], cum_b[C - 1:C, :]
        ka_f = k_ref[sl_a, kc].astype(F32)
        kb_f = k_ref[sl_b, kc].astype(F32)
        q_dec_a = (q_ref[sl_a, kc].astype(F32) * scale * jnp.exp(cum_a)).astype(BF16)
        q_dec_b32 = q_ref[sl_b, kc].astype(F32) * scale * jnp.exp(cum_b)
        q_dec_b = q_dec_b32.astype(BF16)
        k_inv_a = (ka_f * jnp.exp(-cum_a)).astype(BF16)
        k_inv_b = (kb_f * jnp.exp(-cum_b)).astype(BF16)
        k_end_a32 = ka_f * jnp.exp(last_a - cum_a)
        k_end_a = k_end_a32.astype(BF16)
        k_end_b = (kb_f * jnp.exp(last_b - cum_b)).astype(BF16)
        q_cat = jnp.concatenate([q_dec_a, (q_dec_b32 * jnp.exp(last_a)).astype(BF16)], axis=0)
        k_cat = jnp.concatenate([(k_end_a32 * jnp.exp(last_b)).astype(BF16), k_end_b], axis=0)
        v_a, v_b = v_ref[sl_a, vc], v_ref[sl_b, vc]
        yield
        att_aa = jnp.where(causal, dot_nt(q_dec_a, k_inv_a), 0.0).astype(BF16)
        att_bb = jnp.where(causal, dot_nt(q_dec_b, k_inv_b), 0.0).astype(BF16)
        att_ba = dot_nt(q_dec_b, k_end_a).astype(BF16)
        st = st_scr[e]
        inter = dot_nt(q_cat, st.astype(BF16))
        o_a = jnp.dot(att_aa, v_a, preferred_element_type=F32) + inter[:C]
        o_b = (jnp.dot(att_bb, v_b, preferred_element_type=F32)
               + jnp.dot(att_ba, v_a, preferred_element_type=F32) + inter[C:])
        yield
        st_scr[e] = (jnp.exp(last_a + last_b) * st
                     + lax.dot_general(v_ref[sl_ab, vc], k_cat, tn, preferred_element_type=F32))
        o = jnp.concatenate([o_a, o_b], axis=0)
        o = o * lax.rsqrt(jnp.mean(o * o, axis=-1, keepdims=True) + EPS) * hg_ref[e]
        o_ref[sl_ab, vc] = (_silu(r_ref[sl_ab, vc].astype(F32)) * o).astype(o_ref.dtype)
        yield


def _gla_kernel(q_ref, k_ref, v_ref, r_ref, a_ref, wa2_ref, ba_ref, hg_ref, o_ref, st_scr, *, scale):
    @pl.when(pl.program_id(2) == 0)
    def _():
        st_scr[...] = jnp.zeros_like(st_scr)

    z = jnp.dot(a_ref[...], wa2_ref[...], preferred_element_type=F32) + ba_ref[...]
    la = (jnp.minimum(z, 0.0) - jnp.log1p(jnp.exp(-jnp.abs(z)))) * (1.0 / GLA_TAU)

    pos = lax.broadcasted_iota(jnp.int32, la.shape, 0) & (GLA_CHUNK - 1)
    cum = la
    step = 1
    while step < GLA_CHUNK:
        cum = cum + jnp.where(pos >= step, pltpu.roll(cum, step, axis=0), 0.0)
        step *= 2

    heads = [_gla_head(e, cum, q_ref, k_ref, v_ref, r_ref, hg_ref, o_ref, st_scr, scale)
             for e in range(st_scr.shape[0])]
    while heads:
        for gen in list(heads):
            try:
                next(gen)
            except StopIteration:
                heads.remove(gen)


def _gla_core(proj, wa2, ba, head_g, rows):
    B, S, _ = proj.shape
    H = GLA_HEADS
    G = GLA_HEADS_PER_STEP
    dk = wa2.shape[1]
    hk = dk // H
    hv = 2 * hk
    a_blk = (2 * dk + 2 * H * hv) // LANES
    ng = H // G
    return pl.pallas_call(
        functools.partial(_gla_kernel, scale=float(hk) ** -0.5),
        grid=(B, ng, S // rows),
        in_specs=[pl.BlockSpec((None, rows, G * hk), lambda b, h, t: (b, t, h)),
                  pl.BlockSpec((None, rows, G * hk), lambda b, h, t: (b, t, ng + h)),
                  pl.BlockSpec((None, rows, G * hv), lambda b, h, t: (b, t, ng + h)),
                  pl.BlockSpec((None, rows, G * hv), lambda b, h, t: (b, t, 2 * ng + h)),
                  pl.BlockSpec((None, rows, LANES), lambda b, h, t: (b, t, a_blk)),
                  pl.BlockSpec((LANES, G * hk), lambda b, h, t: (0, h)),
                  pl.BlockSpec((1, G * hk), lambda b, h, t: (0, h)),
                  pl.BlockSpec((G, 1, hv), lambda b, h, t: (h, 0, 0))],
        out_specs=pl.BlockSpec((None, rows, G * hv), lambda b, h, t: (b, t, h)),
        out_shape=jax.ShapeDtypeStruct((B, S, H * hv), BF16),
        scratch_shapes=[pltpu.VMEM((G, hv, hk), F32)],
        compiler_params=_params(("arbitrary", "arbitrary", "arbitrary")),
        name="gla_core",
    )(proj, proj, proj, proj, proj, wa2, ba, head_g)


def _out_proj_kernel(a_ref, x_ref, w_ref, g_ref, o_ref):
    y = jnp.dot(a_ref[...], w_ref[...], preferred_element_type=F32)
    o_ref[...] = x_ref[...] + g_ref[...] * y


def _out_proj(a, x, w, gate, tm):
    B, S, D = x.shape
    K = a.shape[2]
    return pl.pallas_call(
        _out_proj_kernel,
        grid=(B, S // tm),
        in_specs=[pl.BlockSpec((None, tm, K), lambda b, s: (b, s, 0)),
                  pl.BlockSpec((None, tm, D), lambda b, s: (b, s, 0)),
                  pl.BlockSpec((K, D), lambda b, s: (0, 0)),
                  pl.BlockSpec((None, 1, D), lambda b, s: (b, 0, 0))],
        out_specs=pl.BlockSpec((None, tm, D), lambda b, s: (b, s, 0)),
        out_shape=jax.ShapeDtypeStruct((B, S, D), F32),
        compiler_params=_params(("arbitrary", "arbitrary")),
        name="out_proj",
    )(a, x, w, gate)


def _split3(w):
    hi = w.astype(BF16).astype(F32)
    lo = (w - hi).astype(BF16).astype(F32)
    lo2 = (w - hi - lo).astype(BF16).astype(F32)
    return hi, lo, lo2


def _lane_pick(v, k):
    return v[:, k:k + 1]


def _route_kernel(x_ref, g_ref, sc_ref, sh_ref, wr_ref, br_ref, xs_ref, info_ref, tab_ref,
                  place_scr, h_scr, aux_scr):
    i = pl.program_id(0)

    @pl.when(i == 0)
    def _():
        place_scr[1] = jnp.zeros(place_scr.shape[1:], place_scr.dtype)
        h_scr[1] = jnp.zeros(h_scr.shape[1:], h_scr.dtype)
        aux_scr[1] = jnp.zeros(aux_scr.shape[1:], aux_scr.dtype)

    def body(cur, prv):
        placing = _place_stage(place_scr[prv], h_scr[prv], aux_scr[prv], xs_ref)
        routing = _route_stage(x_ref, g_ref, sc_ref, sh_ref, wr_ref, br_ref, info_ref, tab_ref,
                               place_scr.shape[1])
        routed = None
        while placing is not None or routing is not None:
            if placing is not None:
                try:
                    next(placing)
                except StopIteration:
                    placing = None
            if routing is not None:
                try:
                    next(routing)
                except StopIteration as done:
                    routed, routing = done.value, None
        place_scr[cur], h_scr[cur], aux_scr[cur] = routed

    @pl.when(i % 2 == 0)
    def _():
        body(0, 1)

    @pl.when(i % 2 == 1)
    def _():
        body(1, 0)


def _route_stage(x_ref, g_ref, sc_ref, sh_ref, wr_ref, br_ref, info_ref, tab_ref, cap):
    tm, D = x_ref.shape
    h = _norm_mod(x_ref[...], g_ref[...], sc_ref[...], sh_ref[...])
    h_hi = h.astype(BF16)
    h_lo = (h - h_hi.astype(F32)).astype(BF16)
    both = jnp.dot(h_hi, wr_ref[...], preferred_element_type=F32)
    logits = (both[:, :LANES] + both[:, LANES:]
              + jnp.dot(h_lo, wr_ref[:, :LANES], preferred_element_type=F32)) + br_ref[...]
    yield
    lane = lax.broadcasted_iota(jnp.int32, logits.shape, 1)
    neg = -jnp.inf
    gmask = lane < N_GROUPS
    gl = jnp.where(gmask, logits, neg)
    gm = jnp.max(gl, axis=-1, keepdims=True)
    gsum = jnp.sum(jnp.where(gmask, jnp.exp(logits - gm), 0.0), axis=-1, keepdims=True)
    g_val = 1.0 / gsum
    g_idx = jnp.min(jnp.where(gl == gm, lane, LANES), axis=-1, keepdims=True)
    lo = N_GROUPS + EXPERTS_PER_GROUP * g_idx
    emask = (lane >= lo) & (lane < lo + EXPERTS_PER_GROUP)
    el = jnp.where(emask, logits, neg)
    m1 = jnp.max(el, axis=-1, keepdims=True)
    i1 = jnp.min(jnp.where(el == m1, lane, LANES), axis=-1, keepdims=True)
    el2 = jnp.where(lane == i1, neg, el)
    m2 = jnp.max(el2, axis=-1, keepdims=True)
    i2 = jnp.min(jnp.where(el2 == m2, lane, LANES), axis=-1, keepdims=True)
    e2 = jnp.exp(m2 - m1)
    w1 = g_val / (1.0 + e2)
    w2 = g_val * e2 / (1.0 + e2)
    yield
    oh1 = lane == (i1 - N_GROUPS)
    oh2 = lane == (i2 - N_GROUPS)
    oh = jnp.where(oh1, 1.0, jnp.where(oh2, 1.0, 0.0))
    ri = lax.broadcasted_iota(jnp.int32, (tm, tm), 0)
    ci = lax.broadcasted_iota(jnp.int32, (tm, tm), 1)
    before = jnp.where(ri > ci, 1.0, 0.0).astype(BF16)
    rank = jnp.dot(before, oh.astype(BF16), preferred_element_type=F32)
    count = jnp.sum(oh, axis=0, keepdims=True)
    units = jnp.floor((count + (MOE_UNIT - 1.0)) * (1.0 / MOE_UNIT))
    ui = lax.broadcasted_iota(jnp.int32, (LANES, LANES), 0)
    uj = lax.broadcasted_iota(jnp.int32, (LANES, LANES), 1)
    earlier = jnp.where(ui < uj, 1.0, 0.0).astype(BF16)
    first_unit = jnp.dot(jnp.broadcast_to(units, (8, LANES)).astype(BF16), earlier,
                         preferred_element_type=F32)[0:1]
    pos = rank + first_unit * MOE_UNIT
    slot1 = jnp.sum(jnp.where(oh1, pos, 0.0), axis=-1, keepdims=True)
    slot2 = jnp.sum(jnp.where(oh2, pos, 0.0), axis=-1, keepdims=True)
    info = jnp.where(lane == 0, w1, 0.0)
    info = jnp.where(lane == 1, w2, info)
    info = jnp.where(lane == 2, slot1, info)
    info = jnp.where(lane == 3, slot2, info)
    info_ref[...] = info
    row8 = lax.broadcasted_iota(jnp.int32, (8, LANES), 0)
    tab = jnp.where(row8 == 0, units, jnp.where(row8 == 1, first_unit, 0.0))
    tab_ref[...] = tab.astype(jnp.int32)
    yield

    info_t = info.T
    s1_row = info_t[2:3, :]
    s2_row = info_t[3:4, :]
    r_idx = lax.broadcasted_iota(jnp.int32, (cap, tm), 0).astype(F32)
    place = jnp.where(r_idx == s1_row, 1.0, jnp.where(r_idx == s2_row, 1.0, 0.0)).astype(BF16)
    w1s = _split3(w1)
    w2s = _split3(w2)
    s1_hi = jnp.floor(slot1 * (1.0 / MOE_UNIT))
    s1_lo = slot1 - s1_hi * MOE_UNIT
    aux = jnp.zeros((tm, LANES), F32)
    for k, col in enumerate(w1s + w2s + (s1_hi, s1_lo)):
        aux = jnp.where(lane == k, col, aux)
    return place, h.astype(BF16), aux.astype(BF16)


def _place_stage(place, h_bf, aux, xs_ref):
    cap, tm = place.shape
    D = h_bf.shape[1]
    cw = 512 if D % 512 == 0 else LANES
    for c0 in range(0, D, cw):
        xs_ref[:, c0:c0 + cw] = jnp.dot(place, h_bf[:, c0:c0 + cw], preferred_element_type=F32).astype(BF16)
        yield
    got = jnp.dot(place, aux, preferred_element_type=F32)
    row_slot1 = _lane_pick(got, 6) * MOE_UNIT + _lane_pick(got, 7)
    r_col = lax.broadcasted_iota(jnp.int32, (cap, 1), 0).astype(F32)
    w_row = jnp.where(row_slot1 == r_col,
                      _lane_pick(got, 0) + _lane_pick(got, 1) + _lane_pick(got, 2),
                      _lane_pick(got, 3) + _lane_pick(got, 4) + _lane_pick(got, 5))
    lane_c = lax.broadcasted_iota(jnp.int32, (cap, LANES), 1)
    wcols = jnp.zeros((cap, LANES), F32)
    for k, col in enumerate(_split3(w_row)):
        wcols = jnp.where(lane_c == k, col, wcols)
    xs_ref[:, D:D + LANES] = wcols.astype(BF16)


def _tile_capacity(tm):
    cap = 2 * tm + N_EXPERTS * (MOE_UNIT - 1)
    return -(-cap // LANES) * LANES


def _norm_route(x, gain, scale, shift, wr, br, tm):
    B, S, D = x.shape
    spt = S // tm
    nt = B * spt
    cap = _tile_capacity(tm)
    routed = lambda i: jnp.minimum(i, nt - 1)
    placed = lambda i: jnp.maximum(i - 1, 0)
    vec = pl.BlockSpec((None, 1, D), lambda i: (routed(i) // spt, 0, 0))
    return pl.pallas_call(
        _route_kernel,
        grid=(nt + 1,),
        in_specs=[pl.BlockSpec((tm, D), lambda i: (routed(i), 0)),
                  pl.BlockSpec((1, D), lambda i: (0, 0)), vec, vec,
                  pl.BlockSpec((D, 2 * LANES), lambda i: (0, 0)),
                  pl.BlockSpec((1, LANES), lambda i: (0, 0))],
        out_specs=[pl.BlockSpec((None, cap, D + LANES), lambda i: (placed(i), 0, 0)),
                   pl.BlockSpec((tm, LANES), lambda i: (routed(i), 0)),
                   pl.BlockSpec((None, 8, LANES), lambda i: (routed(i), 0, 0))],
        out_shape=[jax.ShapeDtypeStruct((nt, cap, D + LANES), BF16),
                   jax.ShapeDtypeStruct((B * S, LANES), F32),
                   jax.ShapeDtypeStruct((nt, 8, LANES), jnp.int32)],
        scratch_shapes=[pltpu.VMEM((2, cap, tm), BF16),
                        pltpu.VMEM((2, tm, D), BF16),
                        pltpu.VMEM((2, tm, LANES), BF16)],
        compiler_params=_params(("arbitrary",)),
        name="norm_route",
    )(x.reshape(B * S, D), gain, scale, shift, wr, br)


def _expert_kernel(be_ref, src_ref, wplan_ref, nused_ref, xs_hbm, w13_hbm, w2_hbm, ys_ref,
                   xbuf, sem, w13f, w2f, wsem, w13b, w2b, *, layer):
    i = pl.program_id(0)
    nb = pl.num_programs(0)
    n_used = nused_ref[0]
    de = w2b.shape[0]
    D = w13b.shape[0]

    def weight_copies(expert, slot):
        return (pltpu.make_async_copy(w13_hbm.at[layer, expert], w13f.at[slot], wsem.at[0, slot]),
                pltpu.make_async_copy(w2_hbm.at[layer, expert], w2f.at[slot], wsem.at[1, slot]))

    def unit_copy(block, u, slot):
        src = pl.multiple_of(src_ref[block * UNITS_PER_BLOCK + u] * MOE_UNIT, MOE_UNIT)
        return pltpu.make_async_copy(xs_hbm.at[pl.ds(src, MOE_UNIT)],
                                     xbuf.at[slot, pl.ds(u * MOE_UNIT, MOE_UNIT)], sem.at[slot])

    def issue(block, slot):
        for u in range(UNITS_PER_BLOCK):
            unit_copy(block, u, slot).start()

    @pl.when((i == 0) & (n_used > 0))
    def _():
        issue(0, 0)

    @pl.when(i + 1 < n_used)
    def _():
        issue(i + 1, (i + 1) % 2)

    @pl.when(i < n_used)
    def _():
        slot = i % 2
        for u in range(UNITS_PER_BLOCK):
            unit_copy(i, u, slot).wait()

        @pl.when(wplan_ref[i] == 1)
        def _():
            expert = be_ref[i]
            wslot = wplan_ref[nb + i]
            nxt = wplan_ref[2 * nb + i]

            @pl.when(i == 0)
            def _():
                for cp in weight_copies(expert, wslot):
                    cp.start()

            for cp in weight_copies(expert, wslot):
                cp.wait()
            w13b[...] = w13f[wslot].astype(BF16)
            w2b[...] = w2f[wslot].astype(BF16)

            @pl.when(nxt >= 0)
            def _():
                for cp in weight_copies(nxt, 1 - wslot):
                    cp.start()

        xb = xbuf[slot, :, 0:D]
        wc = xbuf[slot, :, D:D + LANES].astype(F32)
        w_row = _lane_pick(wc, 0) + _lane_pick(wc, 1) + _lane_pick(wc, 2)
        ag = jnp.dot(xb, w13b[...], preferred_element_type=F32)
        act = (_silu(ag[:, :de]) * ag[:, de:]).astype(BF16)
        ys_ref[...] = (w_row * jnp.dot(act, w2b[...], preferred_element_type=F32)).astype(ys_ref.dtype)

    @pl.when(i >= n_used)
    def _():
        ys_ref[...] = jnp.zeros_like(ys_ref)


def _experts(xs2d, w13, w2, layer, block_expert, src_unit, wplan, n_used):
    D = xs2d.shape[1] - LANES
    n_blocks = block_expert.shape[0]
    de = w2.shape[2]
    grid_spec = pltpu.PrefetchScalarGridSpec(
        num_scalar_prefetch=4,
        grid=(n_blocks,),
        in_specs=[pl.BlockSpec(memory_space=pl.ANY),
                  pl.BlockSpec(memory_space=pl.ANY),
                  pl.BlockSpec(memory_space=pl.ANY)],
        out_specs=pl.BlockSpec((MOE_BLOCK, D), lambda i, be, src, wp, nu: (i, 0)),
        scratch_shapes=[pltpu.VMEM((2, MOE_BLOCK, D + LANES), BF16),
                        pltpu.SemaphoreType.DMA((2,)),
                        pltpu.VMEM((2, D, 2 * de), F32),
                        pltpu.VMEM((2, de, D), F32),
                        pltpu.SemaphoreType.DMA((2, 2)),
                        pltpu.VMEM((D, 2 * de), BF16),
                        pltpu.VMEM((de, D), BF16)])
    return pl.pallas_call(
        functools.partial(_expert_kernel, layer=layer),
        grid_spec=grid_spec,
        out_shape=jax.ShapeDtypeStruct((n_blocks * MOE_BLOCK, D), BF16),
        compiler_params=_params(("arbitrary",)),
        name="moe_experts",
    )(block_expert, src_unit, wplan, n_used, xs2d, w13, w2)


def _combine_kernel(yu_ref, used_ref, ys_hbm, info_ref, x_ref, g_ref, fg_ref, o_ref, ybuf, sem, *, final_norm):
    j = pl.program_id(0)
    nj = pl.num_programs(0)
    tm = x_ref.shape[0]
    cap = ybuf.shape[1]
    n_units = cap // MOE_UNIT

    def unit_copy(tile, q, slot):
        src = pl.multiple_of(yu_ref[tile * n_units + q] * MOE_UNIT, MOE_UNIT)
        dst = pl.multiple_of(q * MOE_UNIT, MOE_UNIT)
        return pltpu.make_async_copy(ys_hbm.at[pl.ds(src, MOE_UNIT)], ybuf.at[slot, pl.ds(dst, MOE_UNIT)], sem.at[slot])

    def issue(tile, slot):
        def body(q, carry):
            unit_copy(tile, q, slot).start()
            return carry
        lax.fori_loop(0, used_ref[tile], body, 0)

    @pl.when(j == 0)
    def _():
        issue(0, 0)

    @pl.when(j + 1 < nj)
    def _():
        issue(j + 1, (j + 1) % 2)

    slot = j % 2
    used = used_ref[j]

    def wait_body(q, carry):
        unit_copy(j, q, slot).wait()
        return carry
    lax.fori_loop(0, used, wait_body, 0)

    def clear_body(q, carry):
        ybuf[slot, pl.ds(pl.multiple_of(q * MOE_UNIT, MOE_UNIT), MOE_UNIT), :] = jnp.zeros((MOE_UNIT, ybuf.shape[2]), ybuf.dtype)
        return carry
    lax.fori_loop(used, n_units, clear_body, 0)

    info = info_ref[...]
    slot1 = _lane_pick(info, 2)
    slot2 = _lane_pick(info, 3)
    r_idx = lax.broadcasted_iota(jnp.int32, (tm, cap), 1).astype(F32)
    pick = jnp.where(r_idx == slot1, 1.0, jnp.where(r_idx == slot2, 1.0, 0.0)).astype(BF16)
    moe = jnp.dot(pick, ybuf[slot], preferred_element_type=F32)
    out = x_ref[...] + g_ref[...] * moe
    if final_norm:
        out = out * lax.rsqrt(jnp.mean(out * out, axis=-1, keepdims=True) + EPS) * fg_ref[...]
    o_ref[...] = out


def _combine(ys, ys_unit, used, info2d, x2d, gate, final_gain, seq, tm, final_norm):
    T, D = x2d.shape
    tiles_per_seq = seq // tm
    cap = _tile_capacity(tm)
    grid_spec = pltpu.PrefetchScalarGridSpec(
        num_scalar_prefetch=2,
        grid=(T // tm,),
        in_specs=[pl.BlockSpec(memory_space=pl.ANY),
                  pl.BlockSpec((tm, LANES), lambda j, yu, us: (j, 0)),
                  pl.BlockSpec((tm, D), lambda j, yu, us: (j, 0)),
                  pl.BlockSpec((None, 1, D), lambda j, yu, us: (j // tiles_per_seq, 0, 0)),
                  pl.BlockSpec((1, D), lambda j, yu, us: (0, 0))],
        out_specs=pl.BlockSpec((tm, D), lambda j, yu, us: (j, 0)),
        scratch_shapes=[pltpu.VMEM((2, cap, D), BF16), pltpu.SemaphoreType.DMA((2,))])
    return pl.pallas_call(
        functools.partial(_combine_kernel, final_norm=final_norm),
        grid_spec=grid_spec,
        out_shape=jax.ShapeDtypeStruct((T, D), F32),
        compiler_params=_params(("arbitrary",)),
        name="moe_combine",
    )(ys_unit, used, ys, info2d, x2d, gate, final_gain)


def _moe_tables(tab, cap, n_blocks):
    nt = tab.shape[0]
    qc = cap // MOE_UNIT
    i32 = jnp.int32
    E = N_EXPERTS
    eids = jnp.arange(E, dtype=i32)

    def pick(onehot, table):
        return jnp.sum(jnp.where(onehot, table, 0), axis=-1)

    units = tab[:, 0, :E]
    first = tab[:, 1, :E]
    cum = jnp.cumsum(units, axis=0)
    excl = cum - units
    ne = cum[-1]
    nb = (ne + UNITS_PER_BLOCK - 1) // UNITS_PER_BLOCK
    bend = jnp.cumsum(nb)
    bstart = bend - nb
    n_used = bend[-1:].astype(i32)
    jb = jnp.arange(n_blocks, dtype=i32)
    block_expert = jnp.minimum(jnp.sum((bend[None, :] <= jb[:, None]).astype(i32), axis=1), E - 1)
    oh_b = block_expert[:, None] == eids[None, :]
    bstart_b = pick(oh_b, bstart[None, :])
    ne_b = pick(oh_b, ne[None, :])
    cum_b = pick(oh_b[:, None, :], cum[None, :, :])
    first_b = pick(oh_b[:, None, :], first[None, :, :])
    excl_b = pick(oh_b[:, None, :], excl[None, :, :])
    u = jnp.arange(UNITS_PER_BLOCK, dtype=i32)
    local = (jb[:, None] - bstart_b[:, None]) * UNITS_PER_BLOCK + u[None, :]
    valid = (jb[:, None] < n_used[0]) & (local < ne_b[:, None])
    t_v = jnp.minimum(jnp.sum((cum_b[:, None, :] <= local[:, :, None]).astype(i32), axis=2), nt - 1)
    oh_t = t_v[:, :, None] == jnp.arange(nt, dtype=i32)[None, None, :]
    src_unit = t_v * qc + pick(oh_t, first_b[:, None, :]) + local - pick(oh_t, excl_b[:, None, :])
    src_unit = jnp.where(valid, src_unit, 0).astype(i32).reshape(-1)

    in_use = jb < n_used[0]
    prev_e = jnp.concatenate([jnp.full((1,), -1, i32), block_expert[:-1]])
    run_first = in_use & (block_expert != prev_e)
    run_slot = (jnp.cumsum(run_first.astype(i32)) - 1) % 2
    run_end = pick(oh_b, bend[None, :])
    next_e = pick(run_end[:, None] == jb[None, :], block_expert[None, :])
    next_e = jnp.where(run_end < n_used[0], next_e, -1)
    wplan = jnp.stack([run_first.astype(i32), run_slot.astype(i32), next_e.astype(i32)], axis=0).reshape(-1)

    q = jnp.arange(qc, dtype=i32)
    gend = first + units
    used = gend[:, -1].astype(i32)
    e_q = jnp.minimum(jnp.sum((gend[:, None, :] <= q[None, :, None]).astype(i32), axis=2), E - 1)
    oh_q = e_q[:, :, None] == eids[None, None, :]
    ys_unit = (UNITS_PER_BLOCK * pick(oh_q, bstart[None, None, :]) + pick(oh_q, excl[:, None, :])
               + q[None, :] - pick(oh_q, first[:, None, :]))
    ys_unit = jnp.where(q[None, :] < used[:, None], ys_unit, 0).astype(i32)
    return block_expert.astype(i32), src_unit, wplan, n_used, ys_unit.reshape(-1), used


def _pick(n, pref):
    t = min(n, pref)
    while n % t:
        t //= 2
    return t


def kernel(x, c, ada_w, ada_b, mix_norm_g, ffn_norm_g, conv_w_pw1, conv_b_pw1, conv_w_dw, conv_b_dw, conv_norm_g, conv_w_pw2, conv_b_pw2, gla_w_in, gla_w_a2, gla_b_a, gla_head_g, gla_w_o, moe_w_group, moe_b_group, moe_w_expert, moe_b_expert, moe_w13, moe_w2, final_norm_g):
    B, S, D = x.shape
    L = ada_w.shape[0]
    T = B * S
    mod = _ada_mod(c, ada_w, ada_b)
    mod = mod.reshape(L, B, N_ADA, 1, D)

    dk = gla_w_a2.shape[2]
    for i in range(L):
        sh1, sc1, g1, sh2, sc2, g2 = [mod[i, :, m] for m in range(N_ADA)]
        j = i // 2
        mg = mix_norm_g[i].reshape(1, D)
        if i % 2 == 0:
            w1 = conv_w_pw1[j].astype(BF16)
            b1 = conv_b_pw1[j].reshape(1, 2 * D)
            u = _norm_glu_proj(x, mg, sc1, sh1, w1[:, :D], w1[:, D:], b1[:, :D], b1[:, D:],
                               tm=_pick(S, 1024), tn=_pick(D, 512))
            x = _conv_tail(u, x, conv_w_dw[j], conv_b_dw[j].reshape(1, D), conv_norm_g[j].reshape(1, D),
                           conv_w_pw2[j].astype(BF16), conv_b_pw2[j].reshape(1, D), g1, tm=_pick(S, 256))
        else:
            w_in = gla_w_in[j]
            n_in = w_in.shape[1]
            tn = PROJ_TILE_N
            n_pad = -(-(n_in - GLA_GATE_RANK + LANES) // tn) * tn
            w_in = jnp.zeros((D, n_pad), F32).at[:, :n_in].set(w_in).astype(BF16)
            proj = _norm_proj(x, mg, sc1, sh1, w_in, tm=_pick(S, 1024), tn=tn)
            wa2 = jnp.zeros((LANES, dk), F32).at[:GLA_GATE_RANK].set(gla_w_a2[j]).astype(BF16)
            gated = _gla_core(proj, wa2, gla_b_a[j].reshape(1, dk),
                              gla_head_g[j].reshape(GLA_HEADS, 1, -1), rows=_pick(S, 512))
            x = _out_proj(gated, x, gla_w_o[j].astype(BF16), g1, tm=_pick(S, 512))

        wr = jnp.zeros((D, LANES), F32).at[:, :N_GROUPS].set(moe_w_group[i])
        wr = wr.at[:, N_GROUPS:N_GROUPS + N_EXPERTS].set(moe_w_expert[i])
        wr_hi = wr.astype(BF16)
        wr_lo = (wr - wr_hi.astype(F32)).astype(BF16)
        wr_cat = jnp.concatenate([wr_hi, wr_lo], axis=1)
        br = jnp.zeros((1, LANES), F32).at[0, :N_GROUPS].set(moe_b_group[i])
        br = br.at[0, N_GROUPS:N_GROUPS + N_EXPERTS].set(moe_b_expert[i])
        tm = _pick(S, MOE_TILE)
        cap = _tile_capacity(tm)
        xs, info, tab = _norm_route(x, ffn_norm_g[i].reshape(1, D), sc2, sh2, wr_cat, br, tm=tm)
        n_blocks = -(-(T // tm) * cap // MOE_BLOCK) + N_EXPERTS
        block_expert, src_unit, wplan, n_used, ys_unit, used = _moe_tables(tab, cap, n_blocks)
        ys = _experts(xs.reshape(-1, D + LANES), moe_w13, moe_w2, i, block_expert, src_unit, wplan, n_used)
        x = _combine(ys, ys_unit, used, info.reshape(T, LANES), x.reshape(T, D), g2, final_norm_g.reshape(1, D),
                     S, tm=tm, final_norm=(i == L - 1)).reshape(B, S, D)

    return x
```

```python
import functools

import jax
import jax.numpy as jnp
from jax import lax
from jax.experimental import pallas as pl
from jax.experimental.pallas import tpu as pltpu

EPS = 1e-6
CONV_WIDTH = 31
GLA_HEADS = 4
GLA_HEADS_PER_STEP = 2
GLA_GATE_RANK = 16
GLA_TAU = 16.0
GLA_CHUNK = 64
N_GROUPS = 4
EXPERTS_PER_GROUP = 8
N_EXPERTS = N_GROUPS * EXPERTS_PER_GROUP
MOE_BLOCK = 256
MOE_UNIT = 16
UNITS_PER_BLOCK = MOE_BLOCK // MOE_UNIT
MOE_TILE = 512
PROJ_TILE_N = 1280
N_ADA = 6
LANES = 128
SUBLANES = 8
VMEM_LIMIT = 56 * 1024 * 1024

F32 = jnp.float32
BF16 = jnp.bfloat16


def _params(sem):
    return pltpu.CompilerParams(dimension_semantics=sem, vmem_limit_bytes=VMEM_LIMIT)


def _norm_mod(x, gain, scale, shift):
    ms = jnp.mean(x * x, axis=-1, keepdims=True)
    return x * lax.rsqrt(ms + EPS) * gain * (1.0 + scale) + shift


def _silu(x):
    return x * jax.nn.sigmoid(x)


def _ada_kernel(c_ref, w_ref, b_ref, o_ref):
    cond = _silu(c_ref[...])
    o_ref[...] = jnp.dot(cond.astype(BF16), w_ref[...].astype(BF16),
                         preferred_element_type=F32) + b_ref[...]


def _ada_mod(c, ada_w, ada_b):
    L, D, N = ada_w.shape
    B = c.shape[0]
    rows = 8
    c_pad = jnp.zeros((rows, D), F32).at[:B].set(c)
    tn = 1024
    out = pl.pallas_call(
        _ada_kernel,
        grid=(L, N // tn),
        in_specs=[pl.BlockSpec((rows, D), lambda l, n: (0, 0)),
                  pl.BlockSpec((None, D, tn), lambda l, n: (l, 0, n)),
                  pl.BlockSpec((None, 1, tn), lambda l, n: (l, 0, n))],
        out_specs=pl.BlockSpec((None, rows, tn), lambda l, n: (l, 0, n)),
        out_shape=jax.ShapeDtypeStruct((L, rows, N), F32),
        compiler_params=_params(("arbitrary", "arbitrary")),
        name="ada_mod",
    )(c_pad, ada_w, ada_b.reshape(L, 1, N))
    return out[:, :B]


def _norm_kernel(x_ref, g_ref, sc_ref, sh_ref, o_ref):
    o_ref[...] = _norm_mod(x_ref[...], g_ref[...], sc_ref[...], sh_ref[...]).astype(o_ref.dtype)


def _mod_norm(x, gain, scale, shift, tm):
    B, S, D = x.shape
    vec = pl.BlockSpec((None, 1, D), lambda b, s: (b, 0, 0))
    return pl.pallas_call(
        _norm_kernel,
        grid=(B, S // tm),
        in_specs=[pl.BlockSpec((None, tm, D), lambda b, s: (b, s, 0)),
                  pl.BlockSpec((1, D), lambda b, s: (0, 0)), vec, vec],
        out_specs=pl.BlockSpec((None, tm, D), lambda b, s: (b, s, 0)),
        out_shape=jax.ShapeDtypeStruct((B, S, D), BF16),
        compiler_params=_params(("arbitrary", "arbitrary")),
        name="mod_norm",
    )(x, gain, scale, shift)


def _proj_kernel(h_ref, w_ref, o_ref):
    o_ref[...] = jnp.dot(h_ref[...], w_ref[...], preferred_element_type=F32).astype(o_ref.dtype)


def _proj(h, w, tm, tn):
    B, S, D = h.shape
    N = w.shape[1]
    return pl.pallas_call(
        _proj_kernel,
        grid=(B, S // tm, N // tn),
        in_specs=[pl.BlockSpec((None, tm, D), lambda b, s, n: (b, s, 0)),
                  pl.BlockSpec((D, tn), lambda b, s, n: (0, n))],
        out_specs=pl.BlockSpec((None, tm, tn), lambda b, s, n: (b, s, n)),
        out_shape=jax.ShapeDtypeStruct((B, S, N), BF16),
        compiler_params=_params(("arbitrary", "arbitrary", "arbitrary")),
        name="proj",
    )(h, w)


def _glu_proj_kernel(h_ref, wa_ref, wb_ref, ba_ref, bb_ref, o_ref):
    h = h_ref[...]
    a = jnp.dot(h, wa_ref[...], preferred_element_type=F32) + ba_ref[...]
    b = jnp.dot(h, wb_ref[...], preferred_element_type=F32) + bb_ref[...]
    o_ref[...] = (a * jax.nn.sigmoid(b)).astype(o_ref.dtype)


def _glu_proj(h, wa, wb, ba, bb, tm, tn):
    B, S, D = h.shape
    N = wa.shape[1]
    return pl.pallas_call(
        _glu_proj_kernel,
        grid=(B, S // tm, N // tn),
        in_specs=[pl.BlockSpec((None, tm, D), lambda b, s, n: (b, s, 0)),
                  pl.BlockSpec((D, tn), lambda b, s, n: (0, n)),
                  pl.BlockSpec((D, tn), lambda b, s, n: (0, n)),
                  pl.BlockSpec((1, tn), lambda b, s, n: (0, n)),
                  pl.BlockSpec((1, tn), lambda b, s, n: (0, n))],
        out_specs=pl.BlockSpec((None, tm, tn), lambda b, s, n: (b, s, n)),
        out_shape=jax.ShapeDtypeStruct((B, S, N), BF16),
        compiler_params=_params(("arbitrary", "arbitrary", "arbitrary")),
        name="glu_proj",
    )(h, wa, wb, ba, bb)


def _conv_kernel(u_ref, x_ref, wdw_ref, bdw_ref, ng_ref, w2_ref, b2_ref, g1_ref, o_ref,
                 uf_scr, p_scr, halo_scr, conv_scr):
    tm, D = conv_scr.shape
    G = tm // SUBLANES
    NW = CONV_WIDTH - 1

    @pl.when(pl.program_id(1) == 0)
    def _():
        halo_scr[...] = jnp.zeros_like(halo_scr)

    def grp(j):
        return slice(j * SUBLANES, (j + 1) * SUBLANES)

    n_strips = D // LANES
    for c in range(n_strips):
        uf_scr[c] = u_ref[:, c * LANES:(c + 1) * LANES].astype(F32)
    sub = lax.broadcasted_iota(jnp.int32, (SUBLANES, LANES), 0)

    def strip(c, carry):
        cols = pl.ds(pl.multiple_of(c * LANES, LANES), LANES)
        for j in range(G):
            p_scr[grp(NW + j), cols] = uf_scr[c, pl.ds(j, SUBLANES, stride=G), :]
        for b in range(NW):
            cur = p_scr[grp(b + G), cols]
            prev = halo_scr[grp(b), cols]
            p_scr[grp(b), cols] = pltpu.roll(jnp.where(sub == SUBLANES - 1, prev, cur), 1, axis=0)
        halo_scr[:, cols] = p_scr[G * SUBLANES:(G + NW) * SUBLANES, cols]
        w = [wdw_ref[k:k + 1, cols] for k in range(CONV_WIDTH)]
        bias = bdw_ref[:, cols]
        for j in range(G):
            acc = p_scr[grp(j), cols] * w[0] + bias
            for k in range(1, CONV_WIDTH):
                acc = acc + p_scr[grp(j + k), cols] * w[k]
            conv_scr[grp(j), cols] = acc
        return carry

    lax.fori_loop(0, n_strips, strip, 0)

    v = conv_scr[...]
    ms = jnp.mean(v * v, axis=-1, keepdims=True)
    v = _silu(v * lax.rsqrt(ms + EPS) * ng_ref[...])
    y = jnp.dot(v.astype(BF16), w2_ref[...], preferred_element_type=F32) + b2_ref[...]
    for c in range(n_strips):
        uf_scr[c] = y[:, c * LANES:(c + 1) * LANES]

    def unstride(c, carry):
        cols = pl.ds(pl.multiple_of(c * LANES, LANES), LANES)
        g1 = g1_ref[:, cols]
        for m in range(G):
            t0 = m * SUBLANES
            start = (t0 % G) * SUBLANES + t0 // G
            o_ref[grp(m), cols] = x_ref[grp(m), cols] + g1 * uf_scr[c, pl.ds(start, SUBLANES, stride=SUBLANES), :]
        return carry

    lax.fori_loop(0, n_strips, unstride, 0)


def _conv_tail(u, x, w_dw, b_dw, norm_g, w2, b2, gate, tm):
    B, S, D = x.shape
    G = tm // SUBLANES
    assert G % SUBLANES == 0 and G >= CONV_WIDTH - 1
    row = pl.BlockSpec((1, D), lambda b, s: (0, 0))
    return pl.pallas_call(
        _conv_kernel,
        grid=(B, S // tm),
        in_specs=[pl.BlockSpec((None, tm, D), lambda b, s: (b, s, 0)),
                  pl.BlockSpec((None, tm, D), lambda b, s: (b, s, 0)),
                  pl.BlockSpec((CONV_WIDTH, D), lambda b, s: (0, 0)),
                  row, row,
                  pl.BlockSpec((D, D), lambda b, s: (0, 0)),
                  row,
                  pl.BlockSpec((None, 1, D), lambda b, s: (b, 0, 0))],
        out_specs=pl.BlockSpec((None, tm, D), lambda b, s: (b, s, 0)),
        out_shape=jax.ShapeDtypeStruct((B, S, D), F32),
        scratch_shapes=[pltpu.VMEM((D // LANES, tm, LANES), F32),
                        pltpu.VMEM((tm + (CONV_WIDTH - 1) * SUBLANES, D), F32),
                        pltpu.VMEM(((CONV_WIDTH - 1) * SUBLANES, D), F32),
                        pltpu.VMEM((tm, D), F32)],
        compiler_params=_params(("arbitrary", "arbitrary")),
        name="conv_tail",
    )(u, x, w_dw, b_dw, norm_g, w2, b2, gate)


def _gla_head(e, cum, q_ref, k_ref, v_ref, r_ref, hg_ref, o_ref, st_scr, scale):
    rows = q_ref.shape[0]
    hv, hk = st_scr.shape[1:]
    kc = slice(e * hk, (e + 1) * hk)
    vc = slice(e * hv, (e + 1) * hv)
    ci = lax.broadcasted_iota(jnp.int32, (GLA_CHUNK, GLA_CHUNK), 0)
    cj = lax.broadcasted_iota(jnp.int32, (GLA_CHUNK, GLA_CHUNK), 1)
    causal = cj <= ci
    nt = (((1,), (1,)), ((), ()))
    tn = (((0,), (0,)), ((), ()))
    C = GLA_CHUNK

    def dot_nt(a, b):
        return lax.dot_general(a, b, nt, preferred_element_type=F32)

    for p in range(rows // (2 * C)):
        sl_a = slice(2 * p * C, (2 * p + 1) * C)
        sl_b = slice((2 * p + 1) * C, (2 * p + 2) * C)
        sl_ab = slice(2 * p * C, (2 * p + 2) * C)
        cum_a, cum_b = cum[sl_a, kc], cum[sl_b, kc]
        last_a, last_b = cum_a[C - 1:C, :], cum_b[C - 1:C, :]
        ka_f = k_ref[sl_a, kc].astype(F32)
        kb_f = k_ref[sl_b, kc].astype(F32)
        q_dec_a = (q_ref[sl_a, kc].astype(F32) * scale * jnp.exp(cum_a)).astype(BF16)
        q_dec_b32 = q_ref[sl_b, kc].astype(F32) * scale * jnp.exp(cum_b)
        q_dec_b = q_dec_b32.astype(BF16)
        k_inv_a = (ka_f * jnp.exp(-cum_a)).astype(BF16)
        k_inv_b = (kb_f * jnp.exp(-cum_b)).astype(BF16)
        k_end_a32 = ka_f * jnp.exp(last_a - cum_a)
        k_end_a = k_end_a32.astype(BF16)
        k_end_b = (kb_f * jnp.exp(last_b - cum_b)).astype(BF16)
        q_cat = jnp.concatenate([q_dec_a, (q_dec_b32 * jnp.exp(last_a)).astype(BF16)], axis=0)
        k_cat = jnp.concatenate([(k_end_a32 * jnp.exp(last_b)).astype(BF16), k_end_b], axis=0)
        v_a, v_b = v_ref[sl_a, vc], v_ref[sl_b, vc]
        yield
        att_aa = jnp.where(causal, dot_nt(q_dec_a, k_inv_a), 0.0).astype(BF16)
        att_bb = jnp.where(causal, dot_nt(q_dec_b, k_inv_b), 0.0).astype(BF16)
        att_ba = dot_nt(q_dec_b, k_end_a).astype(BF16)
        st = st_scr[e]
        inter = dot_nt(q_cat, st.astype(BF16))
        o_a = jnp.dot(att_aa, v_a, preferred_element_type=F32) + inter[:C]
        o_b = (jnp.dot(att_bb, v_b, preferred_element_type=F32)
               + jnp.dot(att_ba, v_a, preferred_element_type=F32) + inter[C:])
        yield
        st_scr[e] = (jnp.exp(last_a + last_b) * st
                     + lax.dot_general(v_ref[sl_ab, vc], k_cat, tn, preferred_element_type=F32))
        o = jnp.concatenate([o_a, o_b], axis=0)
        o = o * lax.rsqrt(jnp.mean(o * o, axis=-1, keepdims=True) + EPS) * hg_ref[e]
        o_ref[sl_ab, vc] = (_silu(r_ref[sl_ab, vc].astype(F32)) * o).astype(o_ref.dtype)
        yield


def _gla_kernel(q_ref, k_ref, v_ref, r_ref, a_ref, wa2_ref, ba_ref, hg_ref, o_ref, st_scr, *, scale):
    @pl.when(pl.program_id(2) == 0)
    def _():
        st_scr[...] = jnp.zeros_like(st_scr)

    z = jnp.dot(a_ref[...], wa2_ref[...], preferred_element_type=F32) + ba_ref[...]
    la = (jnp.minimum(z, 0.0) - jnp.log1p(jnp.exp(-jnp.abs(z)))) * (1.0 / GLA_TAU)

    pos = lax.broadcasted_iota(jnp.int32, la.shape, 0) & (GLA_CHUNK - 1)
    cum = la
    step = 1
    while step < GLA_CHUNK:
        cum = cum + jnp.where(pos >= step, pltpu.roll(cum, step, axis=0), 0.0)
        step *= 2

    heads = [_gla_head(e, cum, q_ref, k_ref, v_ref, r_ref, hg_ref, o_ref, st_scr, scale)
             for e in range(st_scr.shape[0])]
    while heads:
        for gen in list(heads):
            try:
                next(gen)
            except StopIteration:
                heads.remove(gen)


def _gla_core(proj, wa2, ba, head_g, rows):
    B, S, _ = proj.shape
    H = GLA_HEADS
    G = GLA_HEADS_PER_STEP
    dk = wa2.shape[1]
    hk = dk // H
    hv = 2 * hk
    a_blk = (2 * dk + 2 * H * hv) // LANES
    ng = H // G
    return pl.pallas_call(
        functools.partial(_gla_kernel, scale=float(hk) ** -0.5),
        grid=(B, ng, S // rows),
        in_specs=[pl.BlockSpec((None, rows, G * hk), lambda b, h, t: (b, t, h)),
                  pl.BlockSpec((None, rows, G * hk), lambda b, h, t: (b, t, ng + h)),
                  pl.BlockSpec((None, rows, G * hv), lambda b, h, t: (b, t, ng + h)),
                  pl.BlockSpec((None, rows, G * hv), lambda b, h, t: (b, t, 2 * ng + h)),
                  pl.BlockSpec((None, rows, LANES), lambda b, h, t: (b, t, a_blk)),
                  pl.BlockSpec((LANES, G * hk), lambda b, h, t: (0, h)),
                  pl.BlockSpec((1, G * hk), lambda b, h, t: (0, h)),
                  pl.BlockSpec((G, 1, hv), lambda b, h, t: (h, 0, 0))],
        out_specs=pl.BlockSpec((None, rows, G * hv), lambda b, h, t: (b, t, h)),
        out_shape=jax.ShapeDtypeStruct((B, S, H * hv), BF16),
        scratch_shapes=[pltpu.VMEM((G, hv, hk), F32)],
        compiler_params=_params(("arbitrary", "arbitrary", "arbitrary")),
        name="gla_core",
    )(proj, proj, proj, proj, proj, wa2, ba, head_g)


def _out_proj_kernel(a_ref, x_ref, w_ref, g_ref, o_ref):
    y = jnp.dot(a_ref[...], w_ref[...], preferred_element_type=F32)
    o_ref[...] = x_ref[...] + g_ref[...] * y


def _out_proj(a, x, w, gate, tm):
    B, S, D = x.shape
    K = a.shape[2]
    return pl.pallas_call(
        _out_proj_kernel,
        grid=(B, S // tm),
        in_specs=[pl.BlockSpec((None, tm, K), lambda b, s: (b, s, 0)),
                  pl.BlockSpec((None, tm, D), lambda b, s: (b, s, 0)),
                  pl.BlockSpec((K, D), lambda b, s: (0, 0)),
                  pl.BlockSpec((None, 1, D), lambda b, s: (b, 0, 0))],
        out_specs=pl.BlockSpec((None, tm, D), lambda b, s: (b, s, 0)),
        out_shape=jax.ShapeDtypeStruct((B, S, D), F32),
        compiler_params=_params(("arbitrary", "arbitrary")),
        name="out_proj",
    )(a, x, w, gate)


def _split3(w):
    hi = w.astype(BF16).astype(F32)
    lo = (w - hi).astype(BF16).astype(F32)
    lo2 = (w - hi - lo).astype(BF16).astype(F32)
    return hi, lo, lo2


def _lane_pick(v, k):
    return v[:, k:k + 1]


def _route_kernel(x_ref, g_ref, sc_ref, sh_ref, wr_ref, br_ref, xs_ref, info_ref, tab_ref,
                  place_scr, h_scr, aux_scr):
    i = pl.program_id(0)

    @pl.when(i == 0)
    def _():
        place_scr[1] = jnp.zeros(place_scr.shape[1:], place_scr.dtype)
        h_scr[1] = jnp.zeros(h_scr.shape[1:], h_scr.dtype)
        aux_scr[1] = jnp.zeros(aux_scr.shape[1:], aux_scr.dtype)

    def body(cur, prv):
        placing = _place_stage(place_scr[prv], h_scr[prv], aux_scr[prv], xs_ref)
        routing = _route_stage(x_ref, g_ref, sc_ref, sh_ref, wr_ref, br_ref, info_ref, tab_ref,
                               place_scr.shape[1])
        routed = None
        while placing is not None or routing is not None:
            if placing is not None:
                try:
                    next(placing)
                except StopIteration:
                    placing = None
            if routing is not None:
                try:
                    next(routing)
                except StopIteration as done:
                    routed, routing = done.value, None
        place_scr[cur], h_scr[cur], aux_scr[cur] = routed

    @pl.when(i % 2 == 0)
    def _():
        body(0, 1)

    @pl.when(i % 2 == 1)
    def _():
        body(1, 0)


def _route_stage(x_ref, g_ref, sc_ref, sh_ref, wr_ref, br_ref, info_ref, tab_ref, cap):
    tm, D = x_ref.shape
    h = _norm_mod(x_ref[...], g_ref[...], sc_ref[...], sh_ref[...])
    h_hi = h.astype(BF16)
    h_lo = (h - h_hi.astype(F32)).astype(BF16)
    both = jnp.dot(h_hi, wr_ref[...], preferred_element_type=F32)
    logits = (both[:, :LANES] + both[:, LANES:]
              + jnp.dot(h_lo, wr_ref[:, :LANES], preferred_element_type=F32)) + br_ref[...]
    yield
    lane = lax.broadcasted_iota(jnp.int32, logits.shape, 1)
    neg = -jnp.inf
    gmask = lane < N_GROUPS
    gl = jnp.where(gmask, logits, neg)
    gm = jnp.max(gl, axis=-1, keepdims=True)
    gsum = jnp.sum(jnp.where(gmask, jnp.exp(logits - gm), 0.0), axis=-1, keepdims=True)
    g_val = 1.0 / gsum
    g_idx = jnp.min(jnp.where(gl == gm, lane, LANES), axis=-1, keepdims=True)
    lo = N_GROUPS + EXPERTS_PER_GROUP * g_idx
    emask = (lane >= lo) & (lane < lo + EXPERTS_PER_GROUP)
    el = jnp.where(emask, logits, neg)
    m1 = jnp.max(el, axis=-1, keepdims=True)
    i1 = jnp.min(jnp.where(el == m1, lane, LANES), axis=-1, keepdims=True)
    el2 = jnp.where(lane == i1, neg, el)
    m2 = jnp.max(el2, axis=-1, keepdims=True)
    i2 = jnp.min(jnp.where(el2 == m2, lane, LANES), axis=-1, keepdims=True)
    e2 = jnp.exp(m2 - m1)
    w1 = g_val / (1.0 + e2)
    w2 = g_val * e2 / (1.0 + e2)
    yield
    oh1 = lane == (i1 - N_GROUPS)
    oh2 = lane == (i2 - N_GROUPS)
    oh = jnp.where(oh1, 1.0, jnp.where(oh2, 1.0, 0.0))
    ri = lax.broadcasted_iota(jnp.int32, (tm, tm), 0)
    ci = lax.broadcasted_iota(jnp.int32, (tm, tm), 1)
    before = jnp.where(ri > ci, 1.0, 0.0).astype(BF16)
    rank = jnp.dot(before, oh.astype(BF16), preferred_element_type=F32)
    count = jnp.sum(oh, axis=0, keepdims=True)
    units = jnp.floor((count + (MOE_UNIT - 1.0)) * (1.0 / MOE_UNIT))
    ui = lax.broadcasted_iota(jnp.int32, (LANES, LANES), 0)
    uj = lax.broadcasted_iota(jnp.int32, (LANES, LANES), 1)
    earlier = jnp.where(ui < uj, 1.0, 0.0).astype(BF16)
    first_unit = jnp.dot(jnp.broadcast_to(units, (8, LANES)).astype(BF16), earlier,
                         preferred_element_type=F32)[0:1]
    pos = rank + first_unit * MOE_UNIT
    slot1 = jnp.sum(jnp.where(oh1, pos, 0.0), axis=-1, keepdims=True)
    slot2 = jnp.sum(jnp.where(oh2, pos, 0.0), axis=-1, keepdims=True)
    info = jnp.where(lane == 0, w1, 0.0)
    info = jnp.where(lane == 1, w2, info)
    info = jnp.where(lane == 2, slot1, info)
    info = jnp.where(lane == 3, slot2, info)
    info_ref[...] = info
    row8 = lax.broadcasted_iota(jnp.int32, (8, LANES), 0)
    tab = jnp.where(row8 == 0, units, jnp.where(row8 == 1, first_unit, 0.0))
    tab_ref[...] = tab.astype(jnp.int32)
    yield

    info_t = info.T
    s1_row = info_t[2:3, :]
    s2_row = info_t[3:4, :]
    r_idx = lax.broadcasted_iota(jnp.int32, (cap, tm), 0).astype(F32)
    place = jnp.where(r_idx == s1_row, 1.0, jnp.where(r_idx == s2_row, 1.0, 0.0)).astype(BF16)
    w1s = _split3(w1)
    w2s = _split3(w2)
    s1_hi = jnp.floor(slot1 * (1.0 / MOE_UNIT))
    s1_lo = slot1 - s1_hi * MOE_UNIT
    aux = jnp.zeros((tm, LANES), F32)
    for k, col in enumerate(w1s + w2s + (s1_hi, s1_lo)):
        aux = jnp.where(lane == k, col, aux)
    return place, h.astype(BF16), aux.astype(BF16)


def _place_stage(place, h_bf, aux, xs_ref):
    cap, tm = place.shape
    D = h_bf.shape[1]
    cw = 512 if D % 512 == 0 else LANES
    for c0 in range(0, D, cw):
        xs_ref[:, c0:c0 + cw] = jnp.dot(place, h_bf[:, c0:c0 + cw], preferred_element_type=F32).astype(BF16)
        yield
    got = jnp.dot(place, aux, preferred_element_type=F32)
    row_slot1 = _lane_pick(got, 6) * MOE_UNIT + _lane_pick(got, 7)
    r_col = lax.broadcasted_iota(jnp.int32, (cap, 1), 0).astype(F32)
    w_row = jnp.where(row_slot1 == r_col,
                      _lane_pick(got, 0) + _lane_pick(got, 1) + _lane_pick(got, 2),
                      _lane_pick(got, 3) + _lane_pick(got, 4) + _lane_pick(got, 5))
    lane_c = lax.broadcasted_iota(jnp.int32, (cap, LANES), 1)
    wcols = jnp.zeros((cap, LANES), F32)
    for k, col in enumerate(_split3(w_row)):
        wcols = jnp.where(lane_c == k, col, wcols)
    xs_ref[:, D:D + LANES] = wcols.astype(BF16)


def _tile_capacity(tm):
    cap = 2 * tm + N_EXPERTS * (MOE_UNIT - 1)
    return -(-cap // LANES) * LANES


def _norm_route(x, gain, scale, shift, wr, br, tm):
    B, S, D = x.shape
    spt = S // tm
    nt = B * spt
    cap = _tile_capacity(tm)
    routed = lambda i: jnp.minimum(i, nt - 1)
    placed = lambda i: jnp.maximum(i - 1, 0)
    vec = pl.BlockSpec((None, 1, D), lambda i: (routed(i) // spt, 0, 0))
    return pl.pallas_call(
        _route_kernel,
        grid=(nt + 1,),
        in_specs=[pl.BlockSpec((tm, D), lambda i: (routed(i), 0)),
                  pl.BlockSpec((1, D), lambda i: (0, 0)), vec, vec,
                  pl.BlockSpec((D, 2 * LANES), lambda i: (0, 0)),
                  pl.BlockSpec((1, LANES), lambda i: (0, 0))],
        out_specs=[pl.BlockSpec((None, cap, D + LANES), lambda i: (placed(i), 0, 0)),
                   pl.BlockSpec((tm, LANES), lambda i: (routed(i), 0)),
                   pl.BlockSpec((None, 8, LANES), lambda i: (routed(i), 0, 0))],
        out_shape=[jax.ShapeDtypeStruct((nt, cap, D + LANES), BF16),
                   jax.ShapeDtypeStruct((B * S, LANES), F32),
                   jax.ShapeDtypeStruct((nt, 8, LANES), jnp.int32)],
        scratch_shapes=[pltpu.VMEM((2, cap, tm), BF16),
                        pltpu.VMEM((2, tm, D), BF16),
                        pltpu.VMEM((2, tm, LANES), BF16)],
        compiler_params=_params(("arbitrary",)),
        name="norm_route",
    )(x.reshape(B * S, D), gain, scale, shift, wr, br)


def _expert_kernel(be_ref, src_ref, wplan_ref, nused_ref, xs_hbm, w13_hbm, w2_hbm, ys_ref,
                   xbuf, sem, w13f, w2f, wsem, w13b, w2b, *, layer):
    i = pl.program_id(0)
    nb = pl.num_programs(0)
    n_used = nused_ref[0]
    de = w2b.shape[0]
    D = w13b.shape[0]

    def weight_copies(expert, slot):
        return (pltpu.make_async_copy(w13_hbm.at[layer, expert], w13f.at[slot], wsem.at[0, slot]),
                pltpu.make_async_copy(w2_hbm.at[layer, expert], w2f.at[slot], wsem.at[1, slot]))

    def unit_copy(block, u, slot):
        src = pl.multiple_of(src_ref[block * UNITS_PER_BLOCK + u] * MOE_UNIT, MOE_UNIT)
        return pltpu.make_async_copy(xs_hbm.at[pl.ds(src, MOE_UNIT)],
                                     xbuf.at[slot, pl.ds(u * MOE_UNIT, MOE_UNIT)], sem.at[slot])

    def issue(block, slot):
        for u in range(UNITS_PER_BLOCK):
            unit_copy(block, u, slot).start()

    @pl.when((i == 0) & (n_used > 0))
    def _():
        issue(0, 0)

    @pl.when(i + 1 < n_used)
    def _():
        issue(i + 1, (i + 1) % 2)

    @pl.when(i < n_used)
    def _():
        slot = i % 2
        for u in range(UNITS_PER_BLOCK):
            unit_copy(i, u, slot).wait()

        @pl.when(wplan_ref[i] == 1)
        def _():
            expert = be_ref[i]
            wslot = wplan_ref[nb + i]
            nxt = wplan_ref[2 * nb + i]

            @pl.when(i == 0)
            def _():
                for cp in weight_copies(expert, wslot):
                    cp.start()

            for cp in weight_copies(expert, wslot):
                cp.wait()
            w13b[...] = w13f[wslot].astype(BF16)
            w2b[...] = w2f[wslot].astype(BF16)

            @pl.when(nxt >= 0)
            def _():
                for cp in weight_copies(nxt, 1 - wslot):
                    cp.start()

        xb = xbuf[slot, :, 0:D]
        wc = xbuf[slot, :, D:D + LANES].astype(F32)
        w_row = _lane_pick(wc, 0) + _lane_pick(wc, 1) + _lane_pick(wc, 2)
        ag = jnp.dot(xb, w13b[...], preferred_element_type=F32)
        act = (_silu(ag[:, :de]) * ag[:, de:]).astype(BF16)
        ys_ref[...] = (w_row * jnp.dot(act, w2b[...], preferred_element_type=F32)).astype(ys_ref.dtype)

    @pl.when(i >= n_used)
    def _():
        ys_ref[...] = jnp.zeros_like(ys_ref)


def _experts(xs2d, w13, w2, layer, block_expert, src_unit, wplan, n_used):
    D = xs2d.shape[1] - LANES
    n_blocks = block_expert.shape[0]
    de = w2.shape[2]
    grid_spec = pltpu.PrefetchScalarGridSpec(
        num_scalar_prefetch=4,
        grid=(n_blocks,),
        in_specs=[pl.BlockSpec(memory_space=pl.ANY),
                  pl.BlockSpec(memory_space=pl.ANY),
                  pl.BlockSpec(memory_space=pl.ANY)],
        out_specs=pl.BlockSpec((MOE_BLOCK, D), lambda i, be, src, wp, nu: (i, 0)),
        scratch_shapes=[pltpu.VMEM((2, MOE_BLOCK, D + LANES), BF16),
                        pltpu.SemaphoreType.DMA((2,)),
                        pltpu.VMEM((2, D, 2 * de), F32),
                        pltpu.VMEM((2, de, D), F32),
                        pltpu.SemaphoreType.DMA((2, 2)),
                        pltpu.VMEM((D, 2 * de), BF16),
                        pltpu.VMEM((de, D), BF16)])
    return pl.pallas_call(
        functools.partial(_expert_kernel, layer=layer),
        grid_spec=grid_spec,
        out_shape=jax.ShapeDtypeStruct((n_blocks * MOE_BLOCK, D), BF16),
        compiler_params=_params(("arbitrary",)),
        name="moe_experts",
    )(block_expert, src_unit, wplan, n_used, xs2d, w13, w2)


def _combine_kernel(yu_ref, ys_hbm, info_ref, x_ref, g_ref, ng_ref, nsc_ref, nsh_ref, *rest, final_norm):
    if final_norm:
        o_ref, ybuf, sem = rest
    else:
        o_ref, h_ref, ybuf, sem = rest
    j = pl.program_id(0)
    nj = pl.num_programs(0)
    tm = x_ref.shape[0]
    cap = ybuf.shape[1]
    n_units = cap // MOE_UNIT

    def unit_copy(tile, q, slot):
        src = pl.multiple_of(yu_ref[tile * n_units + q] * MOE_UNIT, MOE_UNIT)
        return pltpu.make_async_copy(ys_hbm.at[pl.ds(src, MOE_UNIT)],
                                     ybuf.at[slot, pl.ds(q * MOE_UNIT, MOE_UNIT)], sem.at[slot])

    @pl.when(j == 0)
    def _():
        for q in range(n_units):
            unit_copy(0, q, 0).start()

    nxt = jnp.minimum(j + 1, nj - 1)

    def body(cur, other):
        for q in range(n_units):
            unit_copy(j, q, cur).wait()
        for q in range(n_units):
            unit_copy(nxt, q, other).start()
        info = info_ref[...]
        slot1 = _lane_pick(info, 2)
        slot2 = _lane_pick(info, 3)
        r_idx = lax.broadcasted_iota(jnp.int32, (tm, cap), 1).astype(F32)
        pick = jnp.where(r_idx == slot1, 1.0, jnp.where(r_idx == slot2, 1.0, 0.0)).astype(BF16)
        moe = jnp.dot(pick, ybuf[cur], preferred_element_type=F32)
        out = x_ref[...] + g_ref[...] * moe
        if final_norm:
            o_ref[...] = out * lax.rsqrt(jnp.mean(out * out, axis=-1, keepdims=True) + EPS) * ng_ref[...]
        else:
            o_ref[...] = out
            h_ref[...] = _norm_mod(out, ng_ref[...], nsc_ref[...], nsh_ref[...]).astype(h_ref.dtype)

        @pl.when(j == nj - 1)
        def _():
            for q in range(n_units):
                unit_copy(nxt, q, other).wait()

    @pl.when(j % 2 == 0)
    def _():
        body(0, 1)

    @pl.when(j % 2 == 1)
    def _():
        body(1, 0)


def _combine(ys, ys_unit, info2d, x2d, gate, next_gain, next_scale, next_shift, seq, tm, final_norm):
    T, D = x2d.shape
    tiles_per_seq = seq // tm
    cap = _tile_capacity(tm)
    vec = pl.BlockSpec((None, 1, D), lambda j, yu: (j // tiles_per_seq, 0, 0))
    tile = pl.BlockSpec((tm, D), lambda j, yu: (j, 0))
    grid_spec = pltpu.PrefetchScalarGridSpec(
        num_scalar_prefetch=1,
        grid=(T // tm,),
        in_specs=[pl.BlockSpec(memory_space=pl.ANY),
                  pl.BlockSpec((tm, LANES), lambda j, yu: (j, 0)),
                  tile, vec,
                  pl.BlockSpec((1, D), lambda j, yu: (0, 0)), vec, vec],
        out_specs=tile if final_norm else [tile, tile],
        scratch_shapes=[pltpu.VMEM((2, cap, D), BF16), pltpu.SemaphoreType.DMA((2,))])
    x_shape = jax.ShapeDtypeStruct((T, D), F32)
    return pl.pallas_call(
        functools.partial(_combine_kernel, final_norm=final_norm),
        grid_spec=grid_spec,
        out_shape=x_shape if final_norm else [x_shape, jax.ShapeDtypeStruct((T, D), BF16)],
        compiler_params=_params(("arbitrary",)),
        name="moe_combine",
    )(ys_unit, ys, info2d, x2d, gate, next_gain, next_scale, next_shift)


def _moe_tables(tab, cap, n_blocks):
    nt = tab.shape[0]
    qc = cap // MOE_UNIT
    i32 = jnp.int32
    E = N_EXPERTS
    eids = jnp.arange(E, dtype=i32)

    def pick(onehot, table):
        return jnp.sum(jnp.where(onehot, table, 0), axis=-1)

    units = tab[:, 0, :E]
    first = tab[:, 1, :E]
    cum = jnp.cumsum(units, axis=0)
    excl = cum - units
    ne = cum[-1]
    nb = (ne + UNITS_PER_BLOCK - 1) // UNITS_PER_BLOCK
    bend = jnp.cumsum(nb)
    bstart = bend - nb
    n_used = bend[-1:].astype(i32)
    jb = jnp.arange(n_blocks, dtype=i32)
    block_expert = jnp.minimum(jnp.sum((bend[None, :] <= jb[:, None]).astype(i32), axis=1), E - 1)
    oh_b = block_expert[:, None] == eids[None, :]
    bstart_b = pick(oh_b, bstart[None, :])
    ne_b = pick(oh_b, ne[None, :])
    cum_b = pick(oh_b[:, None, :], cum[None, :, :])
    first_b = pick(oh_b[:, None, :], first[None, :, :])
    excl_b = pick(oh_b[:, None, :], excl[None, :, :])
    u = jnp.arange(UNITS_PER_BLOCK, dtype=i32)
    local = (jb[:, None] - bstart_b[:, None]) * UNITS_PER_BLOCK + u[None, :]
    valid = (jb[:, None] < n_used[0]) & (local < ne_b[:, None])
    t_v = jnp.minimum(jnp.sum((cum_b[:, None, :] <= local[:, :, None]).astype(i32), axis=2), nt - 1)
    oh_t = t_v[:, :, None] == jnp.arange(nt, dtype=i32)[None, None, :]
    src_unit = t_v * qc + pick(oh_t, first_b[:, None, :]) + local - pick(oh_t, excl_b[:, None, :])
    src_unit = jnp.where(valid, src_unit, 0).astype(i32).reshape(-1)

    in_use = jb < n_used[0]
    prev_e = jnp.concatenate([jnp.full((1,), -1, i32), block_expert[:-1]])
    run_first = in_use & (block_expert != prev_e)
    run_slot = (jnp.cumsum(run_first.astype(i32)) - 1) % 2
    run_end = pick(oh_b, bend[None, :])
    next_e = pick(run_end[:, None] == jb[None, :], block_expert[None, :])
    next_e = jnp.where(run_end < n_used[0], next_e, -1)
    wplan = jnp.stack([run_first.astype(i32), run_slot.astype(i32), next_e.astype(i32)], axis=0).reshape(-1)

    q = jnp.arange(qc, dtype=i32)
    gend = first + units
    used = gend[:, -1].astype(i32)
    e_q = jnp.minimum(jnp.sum((gend[:, None, :] <= q[None, :, None]).astype(i32), axis=2), E - 1)
    oh_q = e_q[:, :, None] == eids[None, None, :]
    ys_unit = (UNITS_PER_BLOCK * pick(oh_q, bstart[None, None, :]) + pick(oh_q, excl[:, None, :])
               + q[None, :] - pick(oh_q, first[:, None, :]))
    zero_unit = (n_blocks - 1) * UNITS_PER_BLOCK
    ys_unit = jnp.where(q[None, :] < used[:, None], ys_unit, zero_unit).astype(i32)
    return block_expert.astype(i32), src_unit, wplan, n_used, ys_unit.reshape(-1)


def _pick(n, pref):
    t = min(n, pref)
    while n % t:
        t //= 2
    return t


def kernel(x, c, ada_w, ada_b, mix_norm_g, ffn_norm_g, conv_w_pw1, conv_b_pw1, conv_w_dw, conv_b_dw, conv_norm_g, conv_w_pw2, conv_b_pw2, gla_w_in, gla_w_a2, gla_b_a, gla_head_g, gla_w_o, moe_w_group, moe_b_group, moe_w_expert, moe_b_expert, moe_w13, moe_w2, final_norm_g):
    B, S, D = x.shape
    L = ada_w.shape[0]
    T = B * S
    mod = _ada_mod(c, ada_w, ada_b)
    mod = mod.reshape(L, B, N_ADA, 1, D)

    dk = gla_w_a2.shape[2]
    h = _mod_norm(x, mix_norm_g[0].reshape(1, D), mod[0, :, 1], mod[0, :, 0], tm=_pick(S, 512))
    for i in range(L):
        sh1, sc1, g1, sh2, sc2, g2 = [mod[i, :, m] for m in range(N_ADA)]
        j = i // 2
        if i % 2 == 0:
            w1 = conv_w_pw1[j].astype(BF16)
            b1 = conv_b_pw1[j].reshape(1, 2 * D)
            u = _glu_proj(h, w1[:, :D], w1[:, D:], b1[:, :D], b1[:, D:], tm=_pick(S, 1024), tn=_pick(D, 512))
            x = _conv_tail(u, x, conv_w_dw[j], conv_b_dw[j].reshape(1, D), conv_norm_g[j].reshape(1, D),
                           conv_w_pw2[j].astype(BF16), conv_b_pw2[j].reshape(1, D), g1, tm=_pick(S, 256))
        else:
            w_in = gla_w_in[j]
            n_in = w_in.shape[1]
            tn = PROJ_TILE_N
            n_pad = -(-(n_in - GLA_GATE_RANK + LANES) // tn) * tn
            w_in = jnp.zeros((D, n_pad), F32).at[:, :n_in].set(w_in).astype(BF16)
            proj = _proj(h, w_in, tm=_pick(S, 1024), tn=tn)
            wa2 = jnp.zeros((LANES, dk), F32).at[:GLA_GATE_RANK].set(gla_w_a2[j]).astype(BF16)
            gated = _gla_core(proj, wa2, gla_b_a[j].reshape(1, dk),
                              gla_head_g[j].reshape(GLA_HEADS, 1, -1), rows=_pick(S, 512))
            x = _out_proj(gated, x, gla_w_o[j].astype(BF16), g1, tm=_pick(S, 512))

        wr = jnp.zeros((D, LANES), F32).at[:, :N_GROUPS].set(moe_w_group[i])
        wr = wr.at[:, N_GROUPS:N_GROUPS + N_EXPERTS].set(moe_w_expert[i])
        wr_hi = wr.astype(BF16)
        wr_lo = (wr - wr_hi.astype(F32)).astype(BF16)
        wr_cat = jnp.concatenate([wr_hi, wr_lo], axis=1)
        br = jnp.zeros((1, LANES), F32).at[0, :N_GROUPS].set(moe_b_group[i])
        br = br.at[0, N_GROUPS:N_GROUPS + N_EXPERTS].set(moe_b_expert[i])
        tm = _pick(S, MOE_TILE)
        cap = _tile_capacity(tm)
        xs, info, tab = _norm_route(x, ffn_norm_g[i].reshape(1, D), sc2, sh2, wr_cat, br, tm=tm)
        n_blocks = -(-(T // tm) * cap // MOE_BLOCK) + N_EXPERTS + 1
        block_expert, src_unit, wplan, n_used, ys_unit = _moe_tables(tab, cap, n_blocks)
        ys = _experts(xs.reshape(-1, D + LANES), moe_w13, moe_w2, i, block_expert, src_unit, wplan, n_used)
        if i == L - 1:
            return _combine(ys, ys_unit, info, x.reshape(T, D), g2, final_norm_g.reshape(1, D), sc2, sh2,
                            S, tm=tm, final_norm=True).reshape(B, S, D)
        x, h = _combine(ys, ys_unit, info, x.reshape(T, D), g2, mix_norm_g[i + 1].reshape(1, D),
                        mod[i + 1, :, 1], mod[i + 1, :, 0], S, tm=tm, final_norm=False)
        x = x.reshape(B, S, D)
        h = h.reshape(B, S, D)
```

```python
import functools

import jax
import jax.numpy as jnp
from jax import lax
from jax.experimental import pallas as pl
from jax.experimental.pallas import tpu as pltpu

EPS = 1e-6
CONV_WIDTH = 31
GLA_HEADS = 4
GLA_HEADS_PER_STEP = 2
GLA_GATE_RANK = 16
GLA_TAU = 16.0
GLA_CHUNK = 64
N_GROUPS = 4
EXPERTS_PER_GROUP = 8
N_EXPERTS = N_GROUPS * EXPERTS_PER_GROUP
MOE_BLOCK = 256
MOE_UNIT = 16
UNITS_PER_BLOCK = MOE_BLOCK // MOE_UNIT
MOE_TILE = 512
PROJ_TILE_N = 1280
N_ADA = 6
LANES = 128
SUBLANES = 8
VMEM_LIMIT = 56 * 1024 * 1024

F32 = jnp.float32
BF16 = jnp.bfloat16


def _params(sem):
    return pltpu.CompilerParams(dimension_semantics=sem, vmem_limit_bytes=VMEM_LIMIT)


def _norm_mod(x, gain, scale, shift):
    ms = jnp.mean(x * x, axis=-1, keepdims=True)
    return x * lax.rsqrt(ms + EPS) * gain * (1.0 + scale) + shift


def _silu(x):
    return x * jax.nn.sigmoid(x)


def _ada_kernel(c_ref, w_ref, b_ref, o_ref):
    cond = _silu(c_ref[...])
    o_ref[...] = jnp.dot(cond.astype(BF16), w_ref[...].astype(BF16),
                         preferred_element_type=F32) + b_ref[...]


def _ada_mod(c, ada_w, ada_b):
    L, D, N = ada_w.shape
    B = c.shape[0]
    rows = 8
    c_pad = jnp.zeros((rows, D), F32).at[:B].set(c)
    tn = 1024
    out = pl.pallas_call(
        _ada_kernel,
        grid=(L, N // tn),
        in_specs=[pl.BlockSpec((rows, D), lambda l, n: (0, 0)),
                  pl.BlockSpec((None, D, tn), lambda l, n: (l, 0, n)),
                  pl.BlockSpec((None, 1, tn), lambda l, n: (l, 0, n))],
        out_specs=pl.BlockSpec((None, rows, tn), lambda l, n: (l, 0, n)),
        out_shape=jax.ShapeDtypeStruct((L, rows, N), F32),
        compiler_params=_params(("arbitrary", "arbitrary")),
        name="ada_mod",
    )(c_pad, ada_w, ada_b.reshape(L, 1, N))
    return out[:, :B]


def _norm_kernel(x_ref, g_ref, sc_ref, sh_ref, o_ref):
    o_ref[...] = _norm_mod(x_ref[...], g_ref[...], sc_ref[...], sh_ref[...]).astype(o_ref.dtype)


def _mod_norm(x, gain, scale, shift, tm):
    B, S, D = x.shape
    vec = pl.BlockSpec((None, 1, D), lambda b, s: (b, 0, 0))
    return pl.pallas_call(
        _norm_kernel,
        grid=(B, S // tm),
        in_specs=[pl.BlockSpec((None, tm, D), lambda b, s: (b, s, 0)),
                  pl.BlockSpec((1, D), lambda b, s: (0, 0)), vec, vec],
        out_specs=pl.BlockSpec((None, tm, D), lambda b, s: (b, s, 0)),
        out_shape=jax.ShapeDtypeStruct((B, S, D), BF16),
        compiler_params=_params(("arbitrary", "arbitrary")),
        name="mod_norm",
    )(x, gain, scale, shift)


def _proj_kernel(h_ref, w_ref, o_ref):
    o_ref[...] = jnp.dot(h_ref[...], w_ref[...], preferred_element_type=F32).astype(o_ref.dtype)


def _proj(h, w, tm, tn):
    B, S, D = h.shape
    N = w.shape[1]
    return pl.pallas_call(
        _proj_kernel,
        grid=(B, S // tm, N // tn),
        in_specs=[pl.BlockSpec((None, tm, D), lambda b, s, n: (b, s, 0)),
                  pl.BlockSpec((D, tn), lambda b, s, n: (0, n))],
        out_specs=pl.BlockSpec((None, tm, tn), lambda b, s, n: (b, s, n)),
        out_shape=jax.ShapeDtypeStruct((B, S, N), BF16),
        compiler_params=_params(("arbitrary", "arbitrary", "arbitrary")),
        name="proj",
    )(h, w)


def _glu_proj_kernel(h_ref, wa_ref, wb_ref, ba_ref, bb_ref, o_ref):
    h = h_ref[...]
    a = jnp.dot(h, wa_ref[...], preferred_element_type=F32) + ba_ref[...]
    b = jnp.dot(h, wb_ref[...], preferred_element_type=F32) + bb_ref[...]
    o_ref[...] = (a * jax.nn.sigmoid(b)).astype(o_ref.dtype)


def _glu_proj(h, wa, wb, ba, bb, tm, tn):
    B, S, D = h.shape
    N = wa.shape[1]
    return pl.pallas_call(
        _glu_proj_kernel,
        grid=(B, S // tm, N // tn),
        in_specs=[pl.BlockSpec((None, tm, D), lambda b, s, n: (b, s, 0)),
                  pl.BlockSpec((D, tn), lambda b, s, n: (0, n)),
                  pl.BlockSpec((D, tn), lambda b, s, n: (0, n)),
                  pl.BlockSpec((1, tn), lambda b, s, n: (0, n)),
                  pl.BlockSpec((1, tn), lambda b, s, n: (0, n))],
        out_specs=pl.BlockSpec((None, tm, tn), lambda b, s, n: (b, s, n)),
        out_shape=jax.ShapeDtypeStruct((B, S, N), BF16),
        compiler_params=_params(("arbitrary", "arbitrary", "arbitrary")),
        name="glu_proj",
    )(h, wa, wb, ba, bb)


def _conv_kernel(u_ref, x_ref, wdw_ref, bdw_ref, ng_ref, w2_ref, b2_ref, g1_ref, o_ref,
                 uf_scr, p_scr, halo_scr, conv_scr):
    tm, D = conv_scr.shape
    G = tm // SUBLANES
    NW = CONV_WIDTH - 1

    @pl.when(pl.program_id(1) == 0)
    def _():
        halo_scr[...] = jnp.zeros_like(halo_scr)

    def grp(j):
        return slice(j * SUBLANES, (j + 1) * SUBLANES)

    n_strips = D // LANES
    for c in range(n_strips):
        uf_scr[c] = u_ref[:, c * LANES:(c + 1) * LANES].astype(F32)
    sub = lax.broadcasted_iota(jnp.int32, (SUBLANES, LANES), 0)

    def strip(c, carry):
        cols = pl.ds(pl.multiple_of(c * LANES, LANES), LANES)
        for j in range(G):
            p_scr[grp(NW + j), cols] = uf_scr[c, pl.ds(j, SUBLANES, stride=G), :]
        for b in range(NW):
            cur = p_scr[grp(b + G), cols]
            prev = halo_scr[grp(b), cols]
            p_scr[grp(b), cols] = pltpu.roll(jnp.where(sub == SUBLANES - 1, prev, cur), 1, axis=0)
        halo_scr[:, cols] = p_scr[G * SUBLANES:(G + NW) * SUBLANES, cols]
        w = [wdw_ref[k:k + 1, cols] for k in range(CONV_WIDTH)]
        bias = bdw_ref[:, cols]
        for j in range(G):
            acc = p_scr[grp(j), cols] * w[0] + bias
            for k in range(1, CONV_WIDTH):
                acc = acc + p_scr[grp(j + k), cols] * w[k]
            conv_scr[grp(j), cols] = acc
        return carry

    lax.fori_loop(0, n_strips, strip, 0)

    v = conv_scr[...]
    ms = jnp.mean(v * v, axis=-1, keepdims=True)
    v = _silu(v * lax.rsqrt(ms + EPS) * ng_ref[...])
    y = jnp.dot(v.astype(BF16), w2_ref[...], preferred_element_type=F32) + b2_ref[...]
    for c in range(n_strips):
        uf_scr[c] = y[:, c * LANES:(c + 1) * LANES]

    def unstride(c, carry):
        cols = pl.ds(pl.multiple_of(c * LANES, LANES), LANES)
        g1 = g1_ref[:, cols]
        for m in range(G):
            t0 = m * SUBLANES
            start = (t0 % G) * SUBLANES + t0 // G
            o_ref[grp(m), cols] = x_ref[grp(m), cols] + g1 * uf_scr[c, pl.ds(start, SUBLANES, stride=SUBLANES), :]
        return carry

    lax.fori_loop(0, n_strips, unstride, 0)


def _conv_tail(u, x, w_dw, b_dw, norm_g, w2, b2, gate, tm):
    B, S, D = x.shape
    G = tm // SUBLANES
    assert G % SUBLANES == 0 and G >= CONV_WIDTH - 1
    row = pl.BlockSpec((1, D), lambda b, s: (0, 0))
    return pl.pallas_call(
        _conv_kernel,
        grid=(B, S // tm),
        in_specs=[pl.BlockSpec((None, tm, D), lambda b, s: (b, s, 0)),
                  pl.BlockSpec((None, tm, D), lambda b, s: (b, s, 0)),
                  pl.BlockSpec((CONV_WIDTH, D), lambda b, s: (0, 0)),
                  row, row,
                  pl.BlockSpec((D, D), lambda b, s: (0, 0)),
                  row,
                  pl.BlockSpec((None, 1, D), lambda b, s: (b, 0, 0))],
        out_specs=pl.BlockSpec((None, tm, D), lambda b, s: (b, s, 0)),
        out_shape=jax.ShapeDtypeStruct((B, S, D), F32),
        scratch_shapes=[pltpu.VMEM((D // LANES, tm, LANES), F32),
                        pltpu.VMEM((tm + (CONV_WIDTH - 1) * SUBLANES, D), F32),
                        pltpu.VMEM(((CONV_WIDTH - 1) * SUBLANES, D), F32),
                        pltpu.VMEM((tm, D), F32)],
        compiler_params=_params(("arbitrary", "arbitrary")),
        name="conv_tail",
    )(u, x, w_dw, b_dw, norm_g, w2, b2, gate)


def _gla_head(e, cum, q_ref, k_ref, v_ref, r_ref, hg_ref, o_ref, st_scr, scale):
    rows = q_ref.shape[0]
    hv, hk = st_scr.shape[1:]
    kc = slice(e * hk, (e + 1) * hk)
    vc = slice(e * hv, (e + 1) * hv)
    ci = lax.broadcasted_iota(jnp.int32, (GLA_CHUNK, GLA_CHUNK), 0)
    cj = lax.broadcasted_iota(jnp.int32, (GLA_CHUNK, GLA_CHUNK), 1)
    causal = cj <= ci
    nt = (((1,), (1,)), ((), ()))
    tn = (((0,), (0,)), ((), ()))
    C = GLA_CHUNK

    def dot_nt(a, b):
        return lax.dot_general(a, b, nt, preferred_element_type=F32)

    for p in range(rows // (2 * C)):
        sl_a = slice(2 * p * C, (2 * p + 1) * C)
        sl_b = slice((2 * p + 1) * C, (2 * p + 2) * C)
        sl_ab = slice(2 * p * C, (2 * p + 2) * C)
        cum_a, cum_b = cum[sl_a, kc], cum[sl_b, kc]
        last_a, last_b = cum_a[C - 1:C, :], cum_b[C - 1:C, :]
        ka_f = k_ref[sl_a, kc].astype(F32)
        kb_f = k_ref[sl_b, kc].astype(F32)
        q_dec_a = (q_ref[sl_a, kc].astype(F32) * scale * jnp.exp(cum_a)).astype(BF16)
        q_dec_b32 = q_ref[sl_b, kc].astype(F32) * scale * jnp.exp(cum_b)
        q_dec_b = q_dec_b32.astype(BF16)
        k_inv_a = (ka_f * jnp.exp(-cum_a)).astype(BF16)
        k_inv_b = (kb_f * jnp.exp(-cum_b)).astype(BF16)
        k_end_a32 = ka_f * jnp.exp(last_a - cum_a)
        k_end_a = k_end_a32.astype(BF16)
        k_end_b = (kb_f * jnp.exp(last_b - cum_b)).astype(BF16)
        q_cat = jnp.concatenate([q_dec_a, (q_dec_b32 * jnp.exp(last_a)).astype(BF16)], axis=0)
        k_cat = jnp.concatenate([(k_end_a32 * jnp.exp(last_b)).astype(BF16), k_end_b], axis=0)
        v_a, v_b = v_ref[sl_a, vc], v_ref[sl_b, vc]
        yield
        att_aa = jnp.where(causal, dot_nt(q_dec_a, k_inv_a), 0.0).astype(BF16)
        att_bb = jnp.where(causal, dot_nt(q_dec_b, k_inv_b), 0.0).astype(BF16)
        att_ba = dot_nt(q_dec_b, k_end_a).astype(BF16)
        st = st_scr[e]
        inter = dot_nt(q_cat, st.astype(BF16))
        o_a = jnp.dot(att_aa, v_a, preferred_element_type=F32) + inter[:C]
        o_b = (jnp.dot(att_bb, v_b, preferred_element_type=F32)
               + jnp.dot(att_ba, v_a, preferred_element_type=F32) + inter[C:])
        yield
        st_scr[e] = (jnp.exp(last_a + last_b) * st
                     + lax.dot_general(v_ref[sl_ab, vc], k_cat, tn, preferred_element_type=F32))
        o = jnp.concatenate([o_a, o_b], axis=0)
        o = o * lax.rsqrt(jnp.mean(o * o, axis=-1, keepdims=True) + EPS) * hg_ref[e]
        o_ref[sl_ab, vc] = (_silu(r_ref[sl_ab, vc].astype(F32)) * o).astype(o_ref.dtype)
        yield


def _gla_kernel(q_ref, k_ref, v_ref, r_ref, a_ref, wa2_ref, ba_ref, hg_ref, o_ref, st_scr, *, scale):
    @pl.when(pl.program_id(2) == 0)
    def _():
        st_scr[...] = jnp.zeros_like(st_scr)

    z = jnp.dot(a_ref[...], wa2_ref[...], preferred_element_type=F32) + ba_ref[...]
    la = (jnp.minimum(z, 0.0) - jnp.log1p(jnp.exp(-jnp.abs(z)))) * (1.0 / GLA_TAU)

    pos = lax.broadcasted_iota(jnp.int32, la.shape, 0) & (GLA_CHUNK - 1)
    cum = la
    step = 1
    while step < GLA_CHUNK:
        cum = cum + jnp.where(pos >= step, pltpu.roll(cum, step, axis=0), 0.0)
        step *= 2

    heads = [_gla_head(e, cum, q_ref, k_ref, v_ref, r_ref, hg_ref, o_ref, st_scr, scale)
             for e in range(st_scr.shape[0])]
    while heads:
        for gen in list(heads):
            try:
                next(gen)
            except StopIteration:
                heads.remove(gen)


def _gla_core(proj, wa2, ba, head_g, rows):
    B, S, _ = proj.shape
    H = GLA_HEADS
    G = GLA_HEADS_PER_STEP
    dk = wa2.shape[1]
    hk = dk // H
    hv = 2 * hk
    a_blk = (2 * dk + 2 * H * hv) // LANES
    ng = H // G
    return pl.pallas_call(
        functools.partial(_gla_kernel, scale=float(hk) ** -0.5),
        grid=(B, ng, S // rows),
        in_specs=[pl.BlockSpec((None, rows, G * hk), lambda b, h, t: (b, t, h)),
                  pl.BlockSpec((None, rows, G * hk), lambda b, h, t: (b, t, ng + h)),
                  pl.BlockSpec((None, rows, G * hv), lambda b, h, t: (b, t, ng + h)),
                  pl.BlockSpec((None, rows, G * hv), lambda b, h, t: (b, t, 2 * ng + h)),
                  pl.BlockSpec((None, rows, LANES), lambda b, h, t: (b, t, a_blk)),
                  pl.BlockSpec((LANES, G * hk), lambda b, h, t: (0, h)),
                  pl.BlockSpec((1, G * hk), lambda b, h, t: (0, h)),
                  pl.BlockSpec((G, 1, hv), lambda b, h, t: (h, 0, 0))],
        out_specs=pl.BlockSpec((None, rows, G * hv), lambda b, h, t: (b, t, h)),
        out_shape=jax.ShapeDtypeStruct((B, S, H * hv), BF16),
        scratch_shapes=[pltpu.VMEM((G, hv, hk), F32)],
        compiler_params=_params(("arbitrary", "arbitrary", "arbitrary")),
        name="gla_core",
    )(proj, proj, proj, proj, proj, wa2, ba, head_g)


def _out_proj_kernel(a_ref, x_ref, w_ref, g_ref, o_ref):
    y = jnp.dot(a_ref[...], w_ref[...], preferred_element_type=F32)
    o_ref[...] = x_ref[...] + g_ref[...] * y


def _out_proj(a, x, w, gate, tm):
    B, S, D = x.shape
    K = a.shape[2]
    return pl.pallas_call(
        _out_proj_kernel,
        grid=(B, S // tm),
        in_specs=[pl.BlockSpec((None, tm, K), lambda b, s: (b, s, 0)),
                  pl.BlockSpec((None, tm, D), lambda b, s: (b, s, 0)),
                  pl.BlockSpec((K, D), lambda b, s: (0, 0)),
                  pl.BlockSpec((None, 1, D), lambda b, s: (b, 0, 0))],
        out_specs=pl.BlockSpec((None, tm, D), lambda b, s: (b, s, 0)),
        out_shape=jax.ShapeDtypeStruct((B, S, D), F32),
        compiler_params=_params(("arbitrary", "arbitrary")),
        name="out_proj",
    )(a, x, w, gate)


def _split3(w):
    hi = w.astype(BF16).astype(F32)
    lo = (w - hi).astype(BF16).astype(F32)
    lo2 = (w - hi - lo).astype(BF16).astype(F32)
    return hi, lo, lo2


def _lane_pick(v, k):
    return v[:, k:k + 1]


def _route_kernel(x_ref, g_ref, sc_ref, sh_ref, wr_ref, br_ref, xs_ref, info_ref, tab_ref,
                  place_scr, h_scr, aux_scr):
    i = pl.program_id(0)

    @pl.when(i == 0)
    def _():
        place_scr[1] = jnp.zeros(place_scr.shape[1:], place_scr.dtype)
        h_scr[1] = jnp.zeros(h_scr.shape[1:], h_scr.dtype)
        aux_scr[1] = jnp.zeros(aux_scr.shape[1:], aux_scr.dtype)

    def body(cur, prv):
        placing = _place_stage(place_scr[prv], h_scr[prv], aux_scr[prv], xs_ref)
        routing = _route_stage(x_ref, g_ref, sc_ref, sh_ref, wr_ref, br_ref, info_ref, tab_ref,
                               place_scr.shape[1])
        routed = None
        while placing is not None or routing is not None:
            if placing is not None:
                try:
                    next(placing)
                except StopIteration:
                    placing = None
            if routing is not None:
                try:
                    next(routing)
                except StopIteration as done:
                    routed, routing = done.value, None
        place_scr[cur], h_scr[cur], aux_scr[cur] = routed

    @pl.when(i % 2 == 0)
    def _():
        body(0, 1)

    @pl.when(i % 2 == 1)
    def _():
        body(1, 0)


def _route_stage(x_ref, g_ref, sc_ref, sh_ref, wr_ref, br_ref, info_ref, tab_ref, cap):
    tm, D = x_ref.shape
    h = _norm_mod(x_ref[...], g_ref[...], sc_ref[...], sh_ref[...])
    h_hi = h.astype(BF16)
    h_lo = (h - h_hi.astype(F32)).astype(BF16)
    both = jnp.dot(h_hi, wr_ref[...], preferred_element_type=F32)
    logits = (both[:, :LANES] + both[:, LANES:]
              + jnp.dot(h_lo, wr_ref[:, :LANES], preferred_element_type=F32)) + br_ref[...]
    yield
    lane = lax.broadcasted_iota(jnp.int32, logits.shape, 1)
    neg = -jnp.inf
    gmask = lane < N_GROUPS
    gl = jnp.where(gmask, logits, neg)
    gm = jnp.max(gl, axis=-1, keepdims=True)
    gsum = jnp.sum(jnp.where(gmask, jnp.exp(logits - gm), 0.0), axis=-1, keepdims=True)
    g_val = 1.0 / gsum
    g_idx = jnp.min(jnp.where(gl == gm, lane, LANES), axis=-1, keepdims=True)
    lo = N_GROUPS + EXPERTS_PER_GROUP * g_idx
    emask = (lane >= lo) & (lane < lo + EXPERTS_PER_GROUP)
    el = jnp.where(emask, logits, neg)
    m1 = jnp.max(el, axis=-1, keepdims=True)
    i1 = jnp.min(jnp.where(el == m1, lane, LANES), axis=-1, keepdims=True)
    el2 = jnp.where(lane == i1, neg, el)
    m2 = jnp.max(el2, axis=-1, keepdims=True)
    i2 = jnp.min(jnp.where(el2 == m2, lane, LANES), axis=-1, keepdims=True)
    e2 = jnp.exp(m2 - m1)
    w1 = g_val / (1.0 + e2)
    w2 = g_val * e2 / (1.0 + e2)
    yield
    oh1 = lane == (i1 - N_GROUPS)
    oh2 = lane == (i2 - N_GROUPS)
    oh = jnp.where(oh1, 1.0, jnp.where(oh2, 1.0, 0.0))
    ri = lax.broadcasted_iota(jnp.int32, (tm, tm), 0)
    ci = lax.broadcasted_iota(jnp.int32, (tm, tm), 1)
    before = jnp.where(ri > ci, 1.0, 0.0).astype(BF16)
    rank = jnp.dot(before, oh.astype(BF16), preferred_element_type=F32)
    count = jnp.sum(oh, axis=0, keepdims=True)
    units = jnp.floor((count + (MOE_UNIT - 1.0)) * (1.0 / MOE_UNIT))
    ui = lax.broadcasted_iota(jnp.int32, (LANES, LANES), 0)
    uj = lax.broadcasted_iota(jnp.int32, (LANES, LANES), 1)
    earlier = jnp.where(ui < uj, 1.0, 0.0).astype(BF16)
    first_unit = jnp.dot(jnp.broadcast_to(units, (8, LANES)).astype(BF16), earlier,
                         preferred_element_type=F32)[0:1]
    pos = rank + first_unit * MOE_UNIT
    slot1 = jnp.sum(jnp.where(oh1, pos, 0.0), axis=-1, keepdims=True)
    slot2 = jnp.sum(jnp.where(oh2, pos, 0.0), axis=-1, keepdims=True)
    info = jnp.where(lane == 0, w1, 0.0)
    info = jnp.where(lane == 1, w2, info)
    info = jnp.where(lane == 2, slot1, info)
    info = jnp.where(lane == 3, slot2, info)
    info_ref[...] = info
    row8 = lax.broadcasted_iota(jnp.int32, (8, LANES), 0)
    tab = jnp.where(row8 == 0, units, jnp.where(row8 == 1, first_unit, 0.0))
    tab_ref[...] = tab.astype(jnp.int32)
    yield

    info_t = info.T
    s1_row = info_t[2:3, :]
    s2_row = info_t[3:4, :]
    r_idx = lax.broadcasted_iota(jnp.int32, (cap, tm), 0).astype(F32)
    place = jnp.where(r_idx == s1_row, 1.0, jnp.where(r_idx == s2_row, 1.0, 0.0)).astype(BF16)
    w1s = _split3(w1)
    w2s = _split3(w2)
    s1_hi = jnp.floor(slot1 * (1.0 / MOE_UNIT))
    s1_lo = slot1 - s1_hi * MOE_UNIT
    aux = jnp.zeros((tm, LANES), F32)
    for k, col in enumerate(w1s + w2s + (s1_hi, s1_lo)):
        aux = jnp.where(lane == k, col, aux)
    return place, h.astype(BF16), aux.astype(BF16)


def _place_stage(place, h_bf, aux, xs_ref):
    cap, tm = place.shape
    D = h_bf.shape[1]
    cw = 512 if D % 512 == 0 else LANES
    for c0 in range(0, D, cw):
        xs_ref[:, c0:c0 + cw] = jnp.dot(place, h_bf[:, c0:c0 + cw], preferred_element_type=F32).astype(BF16)
        yield
    got = jnp.dot(place, aux, preferred_element_type=F32)
    row_slot1 = _lane_pick(got, 6) * MOE_UNIT + _lane_pick(got, 7)
    r_col = lax.broadcasted_iota(jnp.int32, (cap, 1), 0).astype(F32)
    w_row = jnp.where(row_slot1 == r_col,
                      _lane_pick(got, 0) + _lane_pick(got, 1) + _lane_pick(got, 2),
                      _lane_pick(got, 3) + _lane_pick(got, 4) + _lane_pick(got, 5))
    lane_c = lax.broadcasted_iota(jnp.int32, (cap, LANES), 1)
    wcols = jnp.zeros((cap, LANES), F32)
    for k, col in enumerate(_split3(w_row)):
        wcols = jnp.where(lane_c == k, col, wcols)
    xs_ref[:, D:D + LANES] = wcols.astype(BF16)


def _tile_capacity(tm):
    cap = 2 * tm + N_EXPERTS * (MOE_UNIT - 1)
    return -(-cap // LANES) * LANES


def _norm_route(x, gain, scale, shift, wr, br, tm):
    B, S, D = x.shape
    spt = S // tm
    nt = B * spt
    cap = _tile_capacity(tm)
    routed = lambda i: jnp.minimum(i, nt - 1)
    placed = lambda i: jnp.maximum(i - 1, 0)
    vec = pl.BlockSpec((None, 1, D), lambda i: (routed(i) // spt, 0, 0))
    return pl.pallas_call(
        _route_kernel,
        grid=(nt + 1,),
        in_specs=[pl.BlockSpec((tm, D), lambda i: (routed(i), 0)),
                  pl.BlockSpec((1, D), lambda i: (0, 0)), vec, vec,
                  pl.BlockSpec((D, 2 * LANES), lambda i: (0, 0)),
                  pl.BlockSpec((1, LANES), lambda i: (0, 0))],
        out_specs=[pl.BlockSpec((None, cap, D + LANES), lambda i: (placed(i), 0, 0)),
                   pl.BlockSpec((tm, LANES), lambda i: (routed(i), 0)),
                   pl.BlockSpec((None, 8, LANES), lambda i: (routed(i), 0, 0))],
        out_shape=[jax.ShapeDtypeStruct((nt, cap, D + LANES), BF16),
                   jax.ShapeDtypeStruct((B * S, LANES), F32),
                   jax.ShapeDtypeStruct((nt, 8, LANES), jnp.int32)],
        scratch_shapes=[pltpu.VMEM((2, cap, tm), BF16),
                        pltpu.VMEM((2, tm, D), BF16),
                        pltpu.VMEM((2, tm, LANES), BF16)],
        compiler_params=_params(("arbitrary",)),
        name="norm_route",
    )(x.reshape(B * S, D), gain, scale, shift, wr, br)


def _expert_kernel(be_ref, src_ref, wplan_ref, nused_ref, xs_hbm, w13_hbm, w2_hbm, ys_ref,
                   xbuf, sem, w13f, w2f, wsem, w13b, w2b, *, layer):
    i = pl.program_id(0)
    nb = pl.num_programs(0)
    n_used = nused_ref[0]
    de = w2b.shape[0]
    D = w13b.shape[0]

    def weight_copies(expert, slot):
        return (pltpu.make_async_copy(w13_hbm.at[layer, expert], w13f.at[slot], wsem.at[0, slot]),
                pltpu.make_async_copy(w2_hbm.at[layer, expert], w2f.at[slot], wsem.at[1, slot]))

    def unit_copy(block, u, slot):
        src = pl.multiple_of(src_ref[block * UNITS_PER_BLOCK + u] * MOE_UNIT, MOE_UNIT)
        return pltpu.make_async_copy(xs_hbm.at[pl.ds(src, MOE_UNIT)],
                                     xbuf.at[slot, pl.ds(u * MOE_UNIT, MOE_UNIT)], sem.at[slot])

    def issue(block, slot):
        for u in range(UNITS_PER_BLOCK):
            unit_copy(block, u, slot).start()

    @pl.when((i == 0) & (n_used > 0))
    def _():
        issue(0, 0)

    @pl.when(i + 1 < n_used)
    def _():
        issue(i + 1, (i + 1) % 2)

    @pl.when(i < n_used)
    def _():
        slot = i % 2
        for u in range(UNITS_PER_BLOCK):
            unit_copy(i, u, slot).wait()

        @pl.when(wplan_ref[i] == 1)
        def _():
            expert = be_ref[i]
            wslot = wplan_ref[nb + i]
            nxt = wplan_ref[2 * nb + i]

            @pl.when(i == 0)
            def _():
                for cp in weight_copies(expert, wslot):
                    cp.start()

            for cp in weight_copies(expert, wslot):
                cp.wait()
            w13b[...] = w13f[wslot].astype(BF16)
            w2b[...] = w2f[wslot].astype(BF16)

            @pl.when(nxt >= 0)
            def _():
                for cp in weight_copies(nxt, 1 - wslot):
                    cp.start()

        xb = xbuf[slot, :, 0:D]
        wc = xbuf[slot, :, D:D + LANES].astype(F32)
        w_row = _lane_pick(wc, 0) + _lane_pick(wc, 1) + _lane_pick(wc, 2)
        ag = jnp.dot(xb, w13b[...], preferred_element_type=F32)
        act = (_silu(ag[:, :de]) * ag[:, de:]).astype(BF16)
        ys_ref[...] = (w_row * jnp.dot(act, w2b[...], preferred_element_type=F32)).astype(ys_ref.dtype)

    @pl.when(i >= n_used)
    def _():
        ys_ref[...] = jnp.zeros_like(ys_ref)


def _experts(xs2d, w13, w2, layer, block_expert, src_unit, wplan, n_used):
    D = xs2d.shape[1] - LANES
    n_blocks = block_expert.shape[0]
    de = w2.shape[2]
    grid_spec = pltpu.PrefetchScalarGridSpec(
        num_scalar_prefetch=4,
        grid=(n_blocks,),
        in_specs=[pl.BlockSpec(memory_space=pl.ANY),
                  pl.BlockSpec(memory_space=pl.ANY),
                  pl.BlockSpec(memory_space=pl.ANY)],
        out_specs=pl.BlockSpec((MOE_BLOCK, D), lambda i, be, src, wp, nu: (i, 0)),
        scratch_shapes=[pltpu.VMEM((2, MOE_BLOCK, D + LANES), BF16),
                        pltpu.SemaphoreType.DMA((2,)),
                        pltpu.VMEM((2, D, 2 * de), F32),
                        pltpu.VMEM((2, de, D), F32),
                        pltpu.SemaphoreType.DMA((2, 2)),
                        pltpu.VMEM((D, 2 * de), BF16),
                        pltpu.VMEM((de, D), BF16)])
    return pl.pallas_call(
        functools.partial(_expert_kernel, layer=layer),
        grid_spec=grid_spec,
        out_shape=jax.ShapeDtypeStruct((n_blocks * MOE_BLOCK, D), BF16),
        compiler_params=_params(("arbitrary",)),
        name="moe_experts",
    )(block_expert, src_unit, wplan, n_used, xs2d, w13, w2)


def _combine_kernel(yu_ref, used_ref, ys_hbm, info_ref, x_ref, g_ref, fg_ref, o_ref, ybuf, sem, *, final_norm):
    j = pl.program_id(0)
    nj = pl.num_programs(0)
    tm = x_ref.shape[0]
    cap = ybuf.shape[1]
    n_units = cap // MOE_UNIT

    def unit_copy(tile, q, slot):
        src = pl.multiple_of(yu_ref[tile * n_units + q] * MOE_UNIT, MOE_UNIT)
        dst = pl.multiple_of(q * MOE_UNIT, MOE_UNIT)
        return pltpu.make_async_copy(ys_hbm.at[pl.ds(src, MOE_UNIT)], ybuf.at[slot, pl.ds(dst, MOE_UNIT)], sem.at[slot])

    def issue(tile, slot):
        def body(q, carry):
            unit_copy(tile, q, slot).start()
            return carry
        lax.fori_loop(0, used_ref[tile], body, 0)

    @pl.when(j == 0)
    def _():
        issue(0, 0)

    @pl.when(j + 1 < nj)
    def _():
        issue(j + 1, (j + 1) % 2)

    slot = j % 2
    used = used_ref[j]

    def wait_body(q, carry):
        unit_copy(j, q, slot).wait()
        return carry
    lax.fori_loop(0, used, wait_body, 0)

    def clear_body(q, carry):
        ybuf[slot, pl.ds(pl.multiple_of(q * MOE_UNIT, MOE_UNIT), MOE_UNIT), :] = jnp.zeros((MOE_UNIT, ybuf.shape[2]), ybuf.dtype)
        return carry
    lax.fori_loop(used, n_units, clear_body, 0)

    info = info_ref[...]
    slot1 = _lane_pick(info, 2)
    slot2 = _lane_pick(info, 3)
    r_idx = lax.broadcasted_iota(jnp.int32, (tm, cap), 1).astype(F32)
    pick = jnp.where(r_idx == slot1, 1.0, jnp.where(r_idx == slot2, 1.0, 0.0)).astype(BF16)
    moe = jnp.dot(pick, ybuf[slot], preferred_element_type=F32)
    out = x_ref[...] + g_ref[...] * moe
    if final_norm:
        out = out * lax.rsqrt(jnp.mean(out * out, axis=-1, keepdims=True) + EPS) * fg_ref[...]
    o_ref[...] = out


def _combine(ys, ys_unit, used, info2d, x2d, gate, final_gain, seq, tm, final_norm):
    T, D = x2d.shape
    tiles_per_seq = seq // tm
    cap = _tile_capacity(tm)
    grid_spec = pltpu.PrefetchScalarGridSpec(
        num_scalar_prefetch=2,
        grid=(T // tm,),
        in_specs=[pl.BlockSpec(memory_space=pl.ANY),
                  pl.BlockSpec((tm, LANES), lambda j, yu, us: (j, 0)),
                  pl.BlockSpec((tm, D), lambda j, yu, us: (j, 0)),
                  pl.BlockSpec((None, 1, D), lambda j, yu, us: (j // tiles_per_seq, 0, 0)),
                  pl.BlockSpec((1, D), lambda j, yu, us: (0, 0))],
        out_specs=pl.BlockSpec((tm, D), lambda j, yu, us: (j, 0)),
        scratch_shapes=[pltpu.VMEM((2, cap, D), BF16), pltpu.SemaphoreType.DMA((2,))])
    return pl.pallas_call(
        functools.partial(_combine_kernel, final_norm=final_norm),
        grid_spec=grid_spec,
        out_shape=jax.ShapeDtypeStruct((T, D), F32),
        compiler_params=_params(("arbitrary",)),
        name="moe_combine",
    )(ys_unit, used, ys, info2d, x2d, gate, final_gain)


def _moe_tables(tab, cap, n_blocks):
    nt = tab.shape[0]
    qc = cap // MOE_UNIT
    i32 = jnp.int32
    E = N_EXPERTS
    eids = jnp.arange(E, dtype=i32)

    def pick(onehot, table):
        return jnp.sum(jnp.where(onehot, table, 0), axis=-1)

    units = tab[:, 0, :E]
    first = tab[:, 1, :E]
    cum = jnp.cumsum(units, axis=0)
    excl = cum - units
    ne = cum[-1]
    nb = (ne + UNITS_PER_BLOCK - 1) // UNITS_PER_BLOCK
    bend = jnp.cumsum(nb)
    bstart = bend - nb
    n_used = bend[-1:].astype(i32)
    jb = jnp.arange(n_blocks, dtype=i32)
    block_expert = jnp.minimum(jnp.sum((bend[None, :] <= jb[:, None]).astype(i32), axis=1), E - 1)
    oh_b = block_expert[:, None] == eids[None, :]
    bstart_b = pick(oh_b, bstart[None, :])
    ne_b = pick(oh_b, ne[None, :])
    cum_b = pick(oh_b[:, None, :], cum[None, :, :])
    first_b = pick(oh_b[:, None, :], first[None, :, :])
    excl_b = pick(oh_b[:, None, :], excl[None, :, :])
    u = jnp.arange(UNITS_PER_BLOCK, dtype=i32)
    local = (jb[:, None] - bstart_b[:, None]) * UNITS_PER_BLOCK + u[None, :]
    valid = (jb[:, None] < n_used[0]) & (local < ne_b[:, None])
    t_v = jnp.minimum(jnp.sum((cum_b[:, None, :] <= local[:, :, None]).astype(i32), axis=2), nt - 1)
    oh_t = t_v[:, :, None] == jnp.arange(nt, dtype=i32)[None, None, :]
    src_unit = t_v * qc + pick(oh_t, first_b[:, None, :]) + local - pick(oh_t, excl_b[:, None, :])
    src_unit = jnp.where(valid, src_unit, 0).astype(i32).reshape(-1)

    in_use = jb < n_used[0]
    prev_e = jnp.concatenate([jnp.full((1,), -1, i32), block_expert[:-1]])
    run_first = in_use & (block_expert != prev_e)
    run_slot = (jnp.cumsum(run_first.astype(i32)) - 1) % 2
    run_end = pick(oh_b, bend[None, :])
    next_e = pick(run_end[:, None] == jb[None, :], block_expert[None, :])
    next_e = jnp.where(run_end < n_used[0], next_e, -1)
    wplan = jnp.stack([run_first.astype(i32), run_slot.astype(i32), next_e.astype(i32)], axis=0).reshape(-1)

    q = jnp.arange(qc, dtype=i32)
    gend = first + units
    used = gend[:, -1].astype(i32)
    e_q = jnp.minimum(jnp.sum((gend[:, None, :] <= q[None, :, None]).astype(i32), axis=2), E - 1)
    oh_q = e_q[:, :, None] == eids[None, None, :]
    ys_unit = (UNITS_PER_BLOCK * pick(oh_q, bstart[None, None, :]) + pick(oh_q, excl[:, None, :])
               + q[None, :] - pick(oh_q, first[:, None, :]))
    ys_unit = jnp.where(q[None, :] < used[:, None], ys_unit, 0).astype(i32)
    return block_expert.astype(i32), src_unit, wplan, n_used, ys_unit.reshape(-1), used


def _pick(n, pref):
    t = min(n, pref)
    while n % t:
        t //= 2
    return t


def kernel(x, c, ada_w, ada_b, mix_norm_g, ffn_norm_g, conv_w_pw1, conv_b_pw1, conv_w_dw, conv_b_dw, conv_norm_g, conv_w_pw2, conv_b_pw2, gla_w_in, gla_w_a2, gla_b_a, gla_head_g, gla_w_o, moe_w_group, moe_b_group, moe_w_expert, moe_b_expert, moe_w13, moe_w2, final_norm_g):
    B, S, D = x.shape
    L = ada_w.shape[0]
    T = B * S
    mod = _ada_mod(c, ada_w, ada_b)
    mod = mod.reshape(L, B, N_ADA, 1, D)

    dk = gla_w_a2.shape[2]
    for i in range(L):
        sh1, sc1, g1, sh2, sc2, g2 = [mod[i, :, m] for m in range(N_ADA)]
        j = i // 2
        h = _mod_norm(x, mix_norm_g[i].reshape(1, D), sc1, sh1, tm=_pick(S, 512))
        if i % 2 == 0:
            w1 = conv_w_pw1[j].astype(BF16)
            b1 = conv_b_pw1[j].reshape(1, 2 * D)
            u = _glu_proj(h, w1[:, :D], w1[:, D:], b1[:, :D], b1[:, D:], tm=_pick(S, 1024), tn=_pick(D, 512))
            x = _conv_tail(u, x, conv_w_dw[j], conv_b_dw[j].reshape(1, D), conv_norm_g[j].reshape(1, D),
                           conv_w_pw2[j].astype(BF16), conv_b_pw2[j].reshape(1, D), g1, tm=_pick(S, 256))
        else:
            w_in = gla_w_in[j]
            n_in = w_in.shape[1]
            tn = PROJ_TILE_N
            n_pad = -(-(n_in - GLA_GATE_RANK + LANES) // tn) * tn
            w_in = jnp.zeros((D, n_pad), F32).at[:, :n_in].set(w_in).astype(BF16)
            proj = _proj(h, w_in, tm=_pick(S, 1024), tn=tn)
            wa2 = jnp.zeros((LANES, dk), F32).at[:GLA_GATE_RANK].set(gla_w_a2[j]).astype(BF16)
            gated = _gla_core(proj, wa2, gla_b_a[j].reshape(1, dk),
                              gla_head_g[j].reshape(GLA_HEADS, 1, -1), rows=_pick(S, 512))
            x = _out_proj(gated, x, gla_w_o[j].astype(BF16), g1, tm=_pick(S, 512))

        wr = jnp.zeros((D, LANES), F32).at[:, :N_GROUPS].set(moe_w_group[i])
        wr = wr.at[:, N_GROUPS:N_GROUPS + N_EXPERTS].set(moe_w_expert[i])
        wr_hi = wr.astype(BF16)
        wr_lo = (wr - wr_hi.astype(F32)).astype(BF16)
        wr_cat = jnp.concatenate([wr_hi, wr_lo], axis=1)
        br = jnp.zeros((1, LANES), F32).at[0, :N_GROUPS].set(moe_b_group[i])
        br = br.at[0, N_GROUPS:N_GROUPS + N_EXPERTS].set(moe_b_expert[i])
        tm = _pick(S, MOE_TILE)
        cap = _tile_capacity(tm)
        xs, info, tab = _norm_route(x, ffn_norm_g[i].reshape(1, D), sc2, sh2, wr_cat, br, tm=tm)
        n_blocks = -(-(T // tm) * cap // MOE_BLOCK) + N_EXPERTS
        block_expert, src_unit, wplan, n_used, ys_unit, used = _moe_tables(tab, cap, n_blocks)
        ys = _experts(xs.reshape(-1, D + LANES), moe_w13, moe_w2, i, block_expert, src_unit, wplan, n_used)
        x = _combine(ys, ys_unit, used, info, x.reshape(T, D), g2, final_norm_g.reshape(1, D),
                     S, tm=tm, final_norm=(i == L - 1)).reshape(B, S, D)

    return x
```
